```python
import jax, jax.numpy as jnp
from jax import lax
import numpy as np

D_MODEL = 1024
BATCH = 8
SEQ = 4096
DEPTH = 1

CHUNK = 64
Q_BLOCK = 128
SB_HEADS = 8
SB_HEAD_DIM = 64
SB_WIDTH = SB_HEADS * SB_HEAD_DIM
GLA_HEADS = 4
GLA_KEY_WIDTH = D_MODEL // 2
GLA_VALUE_WIDTH = D_MODEL
GLA_DK = GLA_KEY_WIDTH // GLA_HEADS
GLA_DV = GLA_VALUE_WIDTH // GLA_HEADS
GLA_GATE_RANK = 16
GLA_GATE_TAU = 16.0
D_FF = 4 * D_MODEL
EPS = 1e-6

SPLIT_SIZES = (SB_WIDTH, SB_WIDTH, SB_WIDTH,
               GLA_KEY_WIDTH, GLA_KEY_WIDTH, GLA_VALUE_WIDTH, GLA_VALUE_WIDTH, GLA_GATE_RANK,
               D_MODEL, D_MODEL)
D_IN = 3 * SB_WIDTH + 2 * GLA_KEY_WIDTH + 2 * GLA_VALUE_WIDTH + GLA_GATE_RANK + 2 * D_MODEL

kernel_name = 'hybrid_stickbreak_gla_sqrelu_block'


def rms_norm(x, g):
    xf = x.astype(jnp.float32)
    y = xf * lax.rsqrt(jnp.mean(xf * xf, axis=-1, keepdims=True) + EPS)
    return (y * g.astype(jnp.float32)).astype(x.dtype)


def split_heads(t, n_heads):
    b, s, _ = t.shape
    return t.reshape(b, s, n_heads, -1).transpose(0, 2, 1, 3)


def merge_heads(t):
    b, h, s, d = t.shape
    return t.transpose(0, 2, 1, 3).reshape(b, s, h * d)


def stick_breaking_attention(q, k, v):
    _, _, s_len, d = q.shape
    scale = d ** -0.5
    outs = []
    for start in range(0, s_len, Q_BLOCK):
        end = start + Q_BLOCK
        kb = k[:, :, :end]
        vb = v[:, :, :end]
        z = jnp.einsum('bhtd,bhsd->bhts', q[:, :, start:end], kb) * scale
        t_idx = start + jnp.arange(Q_BLOCK)[:, None]
        s_idx = jnp.arange(end)[None, :]
        past = s_idx < t_idx
        log_beta = jax.nn.log_sigmoid(z)
        log_fail = jnp.where(past, log_beta - z, 0.0)
        later = lax.cumsum(log_fail, axis=3, reverse=True) - log_fail
        w = jnp.where(past, jnp.exp(log_beta + later), 0.0)
        outs.append(jnp.einsum('bhts,bhsd->bhtd', w, vb))
    return jnp.concatenate(outs, axis=2)


def gla_chunked(q, k, v, log_a):
    b, h, s_len, dk = q.shape
    dv = v.shape[-1]
    nc = s_len // CHUNK

    def to_chunks(t):
        return t.reshape(b, h, nc, CHUNK, t.shape[-1]).transpose(2, 0, 1, 3, 4)

    qc, kc, vc, ac = (to_chunks(t) for t in (q * dk ** -0.5, k, v, log_a))
    bc = jnp.cumsum(ac, axis=3)
    causal = jnp.tril(jnp.ones((CHUNK, CHUNK), dtype=bool))

    def step(state, inp):
        qi, ki, vi, bi = inp
        b_last = bi[:, :, -1:, :]
        o_inter = jnp.einsum('bhtk,bhkv->bhtv', qi * jnp.exp(bi), state)
        diff = bi[:, :, :, None, :] - bi[:, :, None, :, :]
        decay = jnp.exp(jnp.where(causal[:, :, None], diff, -jnp.inf))
        scores = jnp.einsum('bhtk,bhsk,bhtsk->bhts', qi, ki, decay)
        o_intra = jnp.einsum('bhts,bhsv->bhtv', scores, vi)
        new_state = (jnp.exp(b_last[:, :, 0, :, None]) * state
                     + jnp.einsum('bhsk,bhsv->bhkv', ki * jnp.exp(b_last - bi), vi))
        return new_state, o_inter + o_intra

    state0 = jnp.zeros((b, h, dk, dv), jnp.float32)
    _, o = lax.scan(step, state0, (qc, kc, vc, bc))
    return o.transpose(1, 2, 0, 3, 4).reshape(b, h, s_len, dv)


def setup_inputs(seed: int = 0) -> dict:
    key = jax.random.key(seed)
    ks = jax.random.split(key, 16)
    f32 = jnp.float32

    def w(k, shape, fan_in):
        return jax.random.normal(k, shape, f32) * (fan_in ** -0.5)

    def gain(k, shape):
        return 1.0 + 0.02 * jax.random.normal(k, shape, f32)

    return {
        'x': jax.random.normal(ks[0], (BATCH, SEQ, D_MODEL), f32),
        'norm_mix': gain(ks[1], (DEPTH, D_MODEL)),
        'w_in': w(ks[2], (DEPTH, D_MODEL, D_IN), D_MODEL),
        'w_gate_up': w(ks[3], (DEPTH, GLA_GATE_RANK, GLA_KEY_WIDTH), GLA_GATE_RANK),
        'b_gate_up': 0.1 * jax.random.normal(ks[4], (DEPTH, GLA_KEY_WIDTH), f32),
        'gla_norm': gain(ks[5], (DEPTH, GLA_DV)),
        'w_proj_sb': w(ks[6], (DEPTH, SB_WIDTH, D_MODEL), SB_WIDTH),
        'w_proj_gla': w(ks[7], (DEPTH, GLA_VALUE_WIDTH, D_MODEL), GLA_VALUE_WIDTH),
        'w_out': w(ks[8], (DEPTH, D_MODEL, D_MODEL), D_MODEL),
        'norm_mlp': gain(ks[9], (DEPTH, D_MODEL)),
        'w_ff1': w(ks[10], (DEPTH, D_MODEL, D_FF), D_MODEL),
        'w_ff2': w(ks[11], (DEPTH, D_FF, D_MODEL), D_FF),
        'norm_final': gain(ks[12], (D_MODEL,)),
    }


def reference(x, norm_mix, w_in, w_gate_up, b_gate_up, gla_norm, w_proj_sb, w_proj_gla,
              w_out, norm_mlp, w_ff1, w_ff2, norm_final):
    f32 = jnp.float32
    bsz, s_len, _ = x.shape
    split_points = np.cumsum(SPLIT_SIZES)[:-1].tolist()
    for layer in range(DEPTH):
        h = rms_norm(x, norm_mix[layer])
        proj = h @ w_in[layer]
        (sb_q, sb_k, sb_v, g_q, g_k, g_v, g_out, g_low, gate_sb, gate_gla) = jnp.split(
            proj, split_points, axis=-1)

        o_sb = stick_breaking_attention(split_heads(sb_q, SB_HEADS).astype(f32),
                                        split_heads(sb_k, SB_HEADS).astype(f32),
                                        split_heads(sb_v, SB_HEADS).astype(f32))
        o_sb = merge_heads(o_sb).astype(x.dtype)

        log_a = jax.nn.log_sigmoid((g_low @ w_gate_up[layer] + b_gate_up[layer]).astype(f32)) / GLA_GATE_TAU
        o_gla = gla_chunked(split_heads(g_q, GLA_HEADS).astype(f32),
                            split_heads(g_k, GLA_HEADS).astype(f32),
                            split_heads(g_v, GLA_HEADS).astype(f32),
                            split_heads(log_a, GLA_HEADS))
        o_gla = o_gla.transpose(0, 2, 1, 3)
        o_gla = rms_norm(o_gla, gla_norm[layer])
        o_gla = o_gla * jax.nn.silu(g_out.astype(f32).reshape(bsz, s_len, GLA_HEADS, GLA_DV))
        o_gla = o_gla.reshape(bsz, s_len, GLA_VALUE_WIDTH).astype(x.dtype)

        mix = (jax.nn.sigmoid(gate_sb) * (o_sb @ w_proj_sb[layer])
               + jax.nn.sigmoid(gate_gla) * (o_gla @ w_proj_gla[layer]))
        x = x + mix @ w_out[layer]

        h2 = rms_norm(x, norm_mlp[layer])
        x = x + jnp.square(jax.nn.relu(h2 @ w_ff1[layer])) @ w_ff2[layer]
    return rms_norm(x, norm_final)
```

```python
import functools

import jax
import jax.numpy as jnp
from jax import lax
from jax.experimental import pallas as pl
from jax.experimental.pallas import tpu as pltpu

F32 = jnp.float32
BF16 = jnp.bfloat16
EPS = 1e-6

SB_HEADS = 8
SB_DIM = 64
SB_WIDTH = SB_HEADS * SB_DIM
GLA_HEADS = 4
GLA_DK = 128
GLA_DV = 256
GLA_KEY_WIDTH = GLA_HEADS * GLA_DK
GLA_VALUE_WIDTH = GLA_HEADS * GLA_DV
GLA_GATE_RANK = 16
GLA_GATE_TAU = 16.0
GLA_CHUNK = 64
GLA_SUB = 16
LANES = 128

SB_TQ = 128
SB_UNDERFLOW = -110.0

VMEM_LIMIT = 56 * 1024 * 1024

NT_DIMS = (((1,), (1,)), ((), ()))
TN_DIMS = (((0,), (0,)), ((), ()))


def _rms(x, g):
    ms = jnp.mean(x * x, axis=-1, keepdims=True)
    return (x * lax.rsqrt(ms + EPS)) * g


def _log_sigmoid(x):
    return -(jnp.maximum(-x, 0.0) + jnp.log1p(jnp.exp(-jnp.abs(x))))


def _inproj_kernel(x_ref, g_ref, w_ref, sbq_ref, sbk_ref, sbv_ref,
                   gqk_ref, gv_ref, gout_ref, gates_ref, glow_ref):
    h = _rms(x_ref[0], g_ref[...]).astype(BF16)

    def proj(c0, width):
        return jnp.dot(h, w_ref[:, c0:c0 + width], preferred_element_type=F32)

    c = 0
    for idx, ref in enumerate((sbq_ref, sbk_ref, sbv_ref)):
        r = proj(c, SB_WIDTH)
        if idx == 0:
            r = r * (SB_DIM ** -0.5)
        r = r.astype(BF16)
        for hd in range(SB_HEADS):
            ref[0, hd] = r[:, hd * SB_DIM:(hd + 1) * SB_DIM]
        c += SB_WIDTH
    for ref, width in ((gqk_ref, 2 * GLA_KEY_WIDTH), (gv_ref, GLA_VALUE_WIDTH),
                       (gout_ref, GLA_VALUE_WIDTH)):
        ref[0] = proj(c, width).astype(BF16)
        c += width
    d_model = x_ref.shape[-1]
    for half in range(2):
        gates_ref[0, :, half * d_model:(half + 1) * d_model] = proj(c, d_model).astype(BF16)
        c += d_model
    glow_ref[0] = proj(c, LANES).astype(BF16)


def _inproj(x, norm_g, w_all, tm):
    b, s, d = x.shape
    n_all = w_all.shape[1]
    grid = (b, s // tm)
    tok = lambda width: pl.BlockSpec((1, tm, width), lambda bi, i: (bi, i, 0))
    head = pl.BlockSpec((1, SB_HEADS, tm, SB_DIM), lambda bi, i: (bi, 0, i, 0))
    out_shape = (
        jax.ShapeDtypeStruct((b, SB_HEADS, s, SB_DIM), BF16),
        jax.ShapeDtypeStruct((b, SB_HEADS, s, SB_DIM), BF16),
        jax.ShapeDtypeStruct((b, SB_HEADS, s, SB_DIM), BF16),
        jax.ShapeDtypeStruct((b, s, 2 * GLA_KEY_WIDTH), BF16),
        jax.ShapeDtypeStruct((b, s, GLA_VALUE_WIDTH), BF16),
        jax.ShapeDtypeStruct((b, s, GLA_VALUE_WIDTH), BF16),
        jax.ShapeDtypeStruct((b, s, 2 * d), BF16),
        jax.ShapeDtypeStruct((b, s, LANES), BF16),
    )
    return pl.pallas_call(
        _inproj_kernel,
        grid=grid,
        in_specs=[
            tok(d),
            pl.BlockSpec((1, d), lambda bi, i: (0, 0)),
            pl.BlockSpec((d, n_all), lambda bi, i: (0, 0), pipeline_mode=pl.Buffered(1)),
        ],
        out_specs=(head, head, head, tok(2 * GLA_KEY_WIDTH), tok(GLA_VALUE_WIDTH),
                   tok(GLA_VALUE_WIDTH), tok(2 * d), tok(LANES)),
        out_shape=out_shape,
        compiler_params=pltpu.CompilerParams(
            dimension_semantics=("parallel", "parallel"), vmem_limit_bytes=VMEM_LIMIT),
        name="inproj",
    )(x, norm_g, w_all)


def _sb_block(q, kb, vb, carry, tri, mask):
    z = lax.dot_general(q, kb, NT_DIMS, preferred_element_type=F32)
    sp = jnp.maximum(-z, 0.0) + jnp.log1p(jnp.exp(-jnp.abs(z)))
    log_fail = -(sp + z)
    if mask is not None:
        log_fail = jnp.where(mask, log_fail, 0.0)
    hi = log_fail.astype(BF16)
    lo = (log_fail - hi.astype(F32)).astype(BF16)
    later = (jnp.dot(hi, tri, preferred_element_type=F32)
             + jnp.dot(lo, tri, preferred_element_type=F32))
    w = jnp.exp((later + carry) - sp)
    if mask is not None:
        w = jnp.where(mask, w, 0.0)
    contrib = jnp.dot(w.astype(BF16), vb, preferred_element_type=F32)
    return contrib, carry + jnp.sum(log_fail, axis=-1, keepdims=True)


def _sb_kernel(q_ref, k_ref, v_ref, o_ref):
    s_len = q_ref.shape[2]
    n_heads = q_ref.shape[1]
    nq = s_len // SB_TQ
    row = lax.broadcasted_iota(jnp.int32, (SB_TQ, SB_TQ), 0)
    col = lax.broadcasted_iota(jnp.int32, (SB_TQ, SB_TQ), 1)
    tri = (row > col).astype(BF16)
    past = col < row

    for hh in range(n_heads):
        def q_body(i, _, hh=hh):
            q0 = pl.multiple_of(i * SB_TQ, SB_TQ)
            q = q_ref[0, hh, pl.ds(q0, SB_TQ), :]
            kb = k_ref[0, hh, pl.ds(q0, SB_TQ), :]
            vb = v_ref[0, hh, pl.ds(q0, SB_TQ), :]
            acc, carry = _sb_block(q, kb, vb, jnp.zeros((SB_TQ, 1), F32), tri, past)

            def cond(st):
                jj, _, _, cmax = st
                return jnp.logical_and(jj < i, cmax > SB_UNDERFLOW)

            def body(st):
                jj, acc, carry, _ = st
                k0 = pl.multiple_of((i - 1 - jj) * SB_TQ, SB_TQ)
                kb = k_ref[0, hh, pl.ds(k0, SB_TQ), :]
                vb = v_ref[0, hh, pl.ds(k0, SB_TQ), :]
                contrib, carry = _sb_block(q, kb, vb, carry, tri, None)
                return jj + 1, acc + contrib, carry, jnp.max(carry)

            _, acc, _, _ = lax.while_loop(cond, body, (jnp.int32(0), acc, carry, jnp.max(carry)))
            o_ref[0, pl.ds(q0, SB_TQ), hh * SB_DIM:(hh + 1) * SB_DIM] = acc.astype(BF16)
            return 0

        lax.fori_loop(0, nq, q_body, 0)


def _sb_attention(q, k, v):
    b, _, s, _ = q.shape
    hp = LANES // SB_DIM
    spec = pl.BlockSpec((1, hp, s, SB_DIM), lambda bi, h: (bi, h, 0, 0))
    return pl.pallas_call(
        _sb_kernel,
        grid=(b, SB_HEADS // hp),
        in_specs=[spec, spec, spec],
        out_specs=pl.BlockSpec((1, s, LANES), lambda bi, h: (bi, 0, h)),
        out_shape=jax.ShapeDtypeStruct((b, s, SB_WIDTH), BF16),
        compiler_params=pltpu.CompilerParams(
            dimension_semantics=("parallel", "parallel"), vmem_limit_bytes=VMEM_LIMIT),
        name="sb_attention",
    )(q, k, v)


def _rows_bcast(a, offset):
    n_sub = a.shape[0] // GLA_SUB
    return jnp.concatenate(
        [jnp.broadcast_to(a[i * GLA_SUB + offset:i * GLA_SUB + offset + 1, :], (GLA_SUB, a.shape[1]))
         for i in range(n_sub)], axis=0)


def _gla_kernel(q_ref, k_ref, v_ref, go_ref, gl_ref, wup_ref, bup_ref, gn_ref, o_ref, st_ref):
    s_len = q_ref.shape[1]
    C = GLA_CHUNK
    n_sub = C // GLA_SUB
    scale = GLA_DK ** -0.5

    r_cc = lax.broadcasted_iota(jnp.int32, (C, C), 0)
    c_cc = lax.broadcasted_iota(jnp.int32, (C, C), 1)
    ltri = (c_cc <= r_cc).astype(BF16)
    ltri3 = jnp.concatenate([ltri, ltri, ltri], axis=1)
    same_sub = (r_cc // GLA_SUB) == (c_cc // GLA_SUB)
    r_ck = lax.broadcasted_iota(jnp.int32, (C, GLA_DK), 0)
    r_loc = r_ck % GLA_SUB
    sel_r = lax.broadcasted_iota(jnp.int32, (GLA_SUB * GLA_DK, C), 0)
    sel_c = lax.broadcasted_iota(jnp.int32, (GLA_SUB * GLA_DK, C), 1)
    sel = ((sel_r // GLA_DK) == (sel_c % GLA_SUB)).astype(BF16)

    st_ref[...] = jnp.zeros_like(st_ref)

    def chunk(ci, _):
        t0 = pl.multiple_of(ci * C, C)
        q = q_ref[0, pl.ds(t0, C), :].astype(F32) * scale
        k = k_ref[0, pl.ds(t0, C), :].astype(F32)
        v = v_ref[0, pl.ds(t0, C), :]

        x = jnp.dot(gl_ref[0, pl.ds(t0, C), :], wup_ref[...], preferred_element_type=F32) + bup_ref[...]
        log_a = _log_sigmoid(x) * (1.0 / GLA_GATE_TAU)
        hi = log_a.astype(BF16)
        r1 = log_a - hi.astype(F32)
        mid = r1.astype(BF16)
        lo = (r1 - mid.astype(F32)).astype(BF16)
        bcum = jnp.dot(ltri3, jnp.concatenate([hi, mid, lo], axis=0), preferred_element_type=F32)
        b_last = bcum[C - 1:C, :]

        st = st_ref[...]
        o = lax.dot_general((q * jnp.exp(bcum)).astype(BF16), st.astype(BF16), NT_DIMS,
                            preferred_element_type=F32)
        k_dec = (k * jnp.exp(b_last - bcum)).astype(BF16)
        st_ref[...] = st * jnp.exp(b_last) + lax.dot_general(v, k_dec, TN_DIMS, preferred_element_type=F32)

        qn = (q * jnp.exp(bcum - _rows_bcast(bcum, 0))).astype(BF16)
        off = [jnp.zeros((GLA_SUB, C), F32)]
        for i in range(1, n_sub):
            ref_row = bcum[i * GLA_SUB:i * GLA_SUB + 1, :]
            earlier = r_ck < i * GLA_SUB
            kn = jnp.where(earlier, k * jnp.exp(jnp.where(earlier, ref_row - bcum, 0.0)), 0.0).astype(BF16)
            off.append(lax.dot_general(qn[i * GLA_SUB:(i + 1) * GLA_SUB], kn, NT_DIMS,
                                       preferred_element_type=F32))
        off = jnp.concatenate(off, axis=0)
        prods = []
        for s in range(GLA_SUB):
            decay = jnp.exp(jnp.where(r_loc >= s, bcum - _rows_bcast(bcum, s), -jnp.inf))
            prods.append((q * decay * _rows_bcast(k, s)).astype(BF16))
        diag = jnp.dot(jnp.concatenate(prods, axis=1), sel, preferred_element_type=F32)
        scores = jnp.where(same_sub, diag, off)
        o = o + jnp.dot(scores.astype(BF16), v, preferred_element_type=F32)

        y = _rms(o, gn_ref[...])
        g = go_ref[0, pl.ds(t0, C), :].astype(F32)
        o_ref[0, pl.ds(t0, C), :] = (y * (g * jax.nn.sigmoid(g))).astype(BF16)
        return 0

    lax.fori_loop(0, s_len // C, chunk, 0)


def _gla(gqk, gv, gout, glow, wup, bup, gnorm):
    b, s, _ = gv.shape
    return pl.pallas_call(
        _gla_kernel,
        grid=(b, GLA_HEADS),
        in_specs=[
            pl.BlockSpec((1, s, GLA_DK), lambda bi, h: (bi, 0, h)),
            pl.BlockSpec((1, s, GLA_DK), lambda bi, h: (bi, 0, GLA_HEADS + h)),
            pl.BlockSpec((1, s, GLA_DV), lambda bi, h: (bi, 0, h)),
            pl.BlockSpec((1, s, GLA_DV), lambda bi, h: (bi, 0, h)),
            pl.BlockSpec((1, s, LANES), lambda bi, h: (bi, 0, 0)),
            pl.BlockSpec((LANES, GLA_DK), lambda bi, h: (0, h)),
            pl.BlockSpec((1, GLA_DK), lambda bi, h: (0, h)),
            pl.BlockSpec((1, GLA_DV), lambda bi, h: (0, 0)),
        ],
        out_specs=pl.BlockSpec((1, s, GLA_DV), lambda bi, h: (bi, 0, h)),
        out_shape=jax.ShapeDtypeStruct((b, s, GLA_VALUE_WIDTH), BF16),
        scratch_shapes=[pltpu.VMEM((GLA_DV, GLA_DK), F32)],
        compiler_params=pltpu.CompilerParams(
            dimension_semantics=("parallel", "parallel"), vmem_limit_bytes=VMEM_LIMIT),
        name="gla",
    )(gqk, gqk, gv, gout, glow, wup, bup, gnorm)


def _merge_kernel(x_ref, osb_ref, ogla_ref, gates_ref, wpa_ref, wpb_ref, wout_ref, o_ref):
    d = x_ref.shape[-1]
    a = jnp.dot(osb_ref[0], wpa_ref[...], preferred_element_type=F32)
    c = jnp.dot(ogla_ref[0], wpb_ref[...], preferred_element_type=F32)
    ga = jax.nn.sigmoid(gates_ref[0, :, 0:d].astype(F32))
    gb = jax.nn.sigmoid(gates_ref[0, :, d:2 * d].astype(F32))
    mix = ga * a + gb * c
    o_ref[0] = x_ref[0] + jnp.dot(mix.astype(BF16), wout_ref[...], preferred_element_type=F32)


def _merge(x, osb, ogla, gates, wpa, wpb, wout, tm):
    b, s, d = x.shape
    tok = lambda width: pl.BlockSpec((1, tm, width), lambda bi, i: (bi, i, 0))
    full = lambda w: pl.BlockSpec(w.shape, lambda bi, i: (0, 0), pipeline_mode=pl.Buffered(1))
    return pl.pallas_call(
        _merge_kernel,
        grid=(b, s // tm),
        in_specs=[tok(d), tok(SB_WIDTH), tok(GLA_VALUE_WIDTH), tok(2 * d), full(wpa), full(wpb), full(wout)],
        out_specs=tok(d),
        out_shape=jax.ShapeDtypeStruct((b, s, d), F32),
        compiler_params=pltpu.CompilerParams(
            dimension_semantics=("parallel", "parallel"), vmem_limit_bytes=VMEM_LIMIT),
        name="merge",
    )(x, osb, ogla, gates, wpa, wpb, wout)


def _mlp_kernel(x_ref, g_ref, w1_ref, w2_ref, gf_ref, o_ref, *, ff_chunk):
    x = x_ref[0]
    h = _rms(x, g_ref[...]).astype(BF16)
    d_ff = w1_ref.shape[1]
    acc = jnp.zeros_like(x)
    for c in range(d_ff // ff_chunk):
        a = jnp.dot(h, w1_ref[:, c * ff_chunk:(c + 1) * ff_chunk], preferred_element_type=F32)
        a = jnp.square(jnp.maximum(a, 0.0)).astype(BF16)
        acc = acc + jnp.dot(a, w2_ref[c * ff_chunk:(c + 1) * ff_chunk, :], preferred_element_type=F32)
    o_ref[0] = _rms(x + acc, gf_ref[...])


def _mlp(x, g, w1, w2, gf, tm, ff_chunk):
    b, s, d = x.shape
    tok = pl.BlockSpec((1, tm, d), lambda bi, i: (bi, i, 0))
    vec = pl.BlockSpec((1, d), lambda bi, i: (0, 0))
    full = lambda w: pl.BlockSpec(w.shape, lambda bi, i: (0, 0), pipeline_mode=pl.Buffered(1))
    return pl.pallas_call(
        functools.partial(_mlp_kernel, ff_chunk=ff_chunk),
        grid=(b, s // tm),
        in_specs=[tok, vec, full(w1), full(w2), vec],
        out_specs=tok,
        out_shape=jax.ShapeDtypeStruct((b, s, d), F32),
        compiler_params=pltpu.CompilerParams(
            dimension_semantics=("parallel", "parallel"), vmem_limit_bytes=VMEM_LIMIT),
        name="mlp",
    )(x, g, w1, w2, gf)


def _layer(x, norm_mix, w_in, w_gate_up, b_gate_up, gla_norm, w_proj_sb, w_proj_gla, w_out,
           norm_mlp, w_ff1, w_ff2, final_gain, tm):
    d = x.shape[-1]
    c_low = 3 * SB_WIDTH + 2 * GLA_KEY_WIDTH + 2 * GLA_VALUE_WIDTH
    w_main = jnp.concatenate(
        [w_in[:, :c_low], w_in[:, c_low + GLA_GATE_RANK:],
         w_in[:, c_low:c_low + GLA_GATE_RANK],
         jnp.zeros((d, LANES - GLA_GATE_RANK), w_in.dtype)], axis=1).astype(BF16)
    wup = jnp.concatenate(
        [w_gate_up, jnp.zeros((LANES - GLA_GATE_RANK, GLA_KEY_WIDTH), w_gate_up.dtype)], axis=0).astype(BF16)

    sbq, sbk, sbv, gqk, gv, gout, gates, glow = _inproj(x, norm_mix.reshape(1, d), w_main, tm)
    o_sb = _sb_attention(sbq, sbk, sbv)
    o_gla = _gla(gqk, gv, gout, glow, wup, b_gate_up.reshape(1, -1), gla_norm.reshape(1, -1))
    x = _merge(x, o_sb, o_gla, gates, w_proj_sb.astype(BF16), w_proj_gla.astype(BF16),
               w_out.astype(BF16), tm)
    return _mlp(x, norm_mlp.reshape(1, d), w_ff1.astype(BF16), w_ff2.astype(BF16), final_gain, tm,
                ff_chunk=1024)


def kernel(x, norm_mix, w_in, w_gate_up, b_gate_up, gla_norm, w_proj_sb, w_proj_gla, w_out,
           norm_mlp, w_ff1, w_ff2, norm_final):
    depth = w_in.shape[0]
    s = x.shape[1]
    tm = min(256, s)
    assert depth == 1, "the final RMSNorm is fused into the last layer's MLP kernel"
    return _layer(x, norm_mix[0], w_in[0], w_gate_up[0], b_gate_up[0], gla_norm[0], w_proj_sb[0],
                  w_proj_gla[0], w_out[0], norm_mlp[0], w_ff1[0], w_ff2[0],
                  norm_final.reshape(1, -1), tm)
```

```python
import functools

import jax
import jax.numpy as jnp
from jax import lax
from jax.experimental import pallas as pl
from jax.experimental.pallas import tpu as pltpu

F32 = jnp.float32
BF16 = jnp.bfloat16
EPS = 1e-6

SB_HEADS = 8
SB_DIM = 64
SB_WIDTH = SB_HEADS * SB_DIM
GLA_HEADS = 4
GLA_DK = 128
GLA_DV = 256
GLA_KEY_WIDTH = GLA_HEADS * GLA_DK
GLA_VALUE_WIDTH = GLA_HEADS * GLA_DV
GLA_GATE_RANK = 16
GLA_GATE_TAU = 16.0
GLA_CHUNK = 64
GLA_SUB = 16
GLA_GROUP = 4
LANES = 128

SB_TQ = 128
SB_WINDOW = 3
SB_QB = 2
SB_UNDERFLOW = -110.0

VMEM_LIMIT = 56 * 1024 * 1024

NT_DIMS = (((1,), (1,)), ((), ()))
TN_DIMS = (((0,), (0,)), ((), ()))


def _rms(x, g):
    ms = jnp.mean(x * x, axis=-1, keepdims=True)
    return (x * lax.rsqrt(ms + EPS)) * g


def _log_sigmoid(x):
    return -(jnp.maximum(-x, 0.0) + jnp.log(1.0 + jnp.exp(-jnp.abs(x))))


def _inproj_kernel(x_ref, g_ref, w_ref, sbq_ref, sbk_ref, sbv_ref,
                   gqk_ref, gv_ref, gout_ref, gates_ref, glow_ref):
    h = _rms(x_ref[0], g_ref[...]).astype(BF16)

    def proj(c0, width):
        return jnp.dot(h, w_ref[:, c0:c0 + width], preferred_element_type=F32)

    c = 0
    for idx, ref in enumerate((sbq_ref, sbk_ref, sbv_ref)):
        r = proj(c, SB_WIDTH)
        if idx == 0:
            r = r * (SB_DIM ** -0.5)
        r = r.astype(BF16)
        for hd in range(SB_HEADS):
            ref[0, hd] = r[:, hd * SB_DIM:(hd + 1) * SB_DIM]
        c += SB_WIDTH
    for ref, width in ((gqk_ref, 2 * GLA_KEY_WIDTH), (gv_ref, GLA_VALUE_WIDTH),
                       (gout_ref, GLA_VALUE_WIDTH)):
        ref[0] = proj(c, width).astype(BF16)
        c += width
    d_model = x_ref.shape[-1]
    for half in range(2):
        gates_ref[0, :, half * d_model:(half + 1) * d_model] = proj(c, d_model).astype(BF16)
        c += d_model
    glow_ref[0] = proj(c, LANES).astype(BF16)


def _inproj(x, norm_g, w_all, tm):
    b, s, d = x.shape
    n_all = w_all.shape[1]
    grid = (b, s // tm)
    tok = lambda width: pl.BlockSpec((1, tm, width), lambda bi, i: (bi, i, 0))
    head = pl.BlockSpec((1, SB_HEADS, tm, SB_DIM), lambda bi, i: (bi, 0, i, 0))
    out_shape = (
        jax.ShapeDtypeStruct((b, SB_HEADS, s, SB_DIM), BF16),
        jax.ShapeDtypeStruct((b, SB_HEADS, s, SB_DIM), BF16),
        jax.ShapeDtypeStruct((b, SB_HEADS, s, SB_DIM), BF16),
        jax.ShapeDtypeStruct((b, s, 2 * GLA_KEY_WIDTH), BF16),
        jax.ShapeDtypeStruct((b, s, GLA_VALUE_WIDTH), BF16),
        jax.ShapeDtypeStruct((b, s, GLA_VALUE_WIDTH), BF16),
        jax.ShapeDtypeStruct((b, s, 2 * d), BF16),
        jax.ShapeDtypeStruct((b, s, LANES), BF16),
    )
    return pl.pallas_call(
        _inproj_kernel,
        grid=grid,
        in_specs=[
            tok(d),
            pl.BlockSpec((1, d), lambda bi, i: (0, 0)),
            pl.BlockSpec((d, n_all), lambda bi, i: (0, 0), pipeline_mode=pl.Buffered(1)),
        ],
        out_specs=(head, head, head, tok(2 * GLA_KEY_WIDTH), tok(GLA_VALUE_WIDTH),
                   tok(GLA_VALUE_WIDTH), tok(2 * d), tok(LANES)),
        out_shape=out_shape,
        compiler_params=pltpu.CompilerParams(
            dimension_semantics=("parallel", "parallel"), vmem_limit_bytes=VMEM_LIMIT),
        name="inproj",
    )(x, norm_g, w_all)


def _aligned(start, tile):
    return start if isinstance(start, int) else pl.multiple_of(start, tile)


def _sb_tile(z, neg_tri2, mask):
    tk = z.shape[1]
    f = jnp.maximum(z, 0.0) + jnp.log(1.0 + jnp.exp(-jnp.abs(z)))
    sneg = f - z
    if mask is not None:
        f = jnp.where(mask, f, 0.0)
    hi = f.astype(BF16)
    lo = (f - hi.astype(F32)).astype(BF16)
    cs = jnp.dot(jnp.concatenate([hi, lo], axis=1), neg_tri2, preferred_element_type=F32)
    return sneg, cs[:, :tk], cs[:, tk:]


def _sb_kernel(q_ref, k_ref, v_ref, o_ref):
    s_len = q_ref.shape[2]
    heads = range(q_ref.shape[1])
    T = SB_TQ
    nq = s_len // T
    row = lax.broadcasted_iota(jnp.int32, (T, T), 0)
    col = lax.broadcasted_iota(jnp.int32, (T, T), 1)
    neg_tri = jnp.where(row > col, -1.0, 0.0).astype(BF16)
    half = jnp.concatenate([neg_tri, jnp.full((T, T), -1.0, BF16)], axis=1)
    neg_tri2 = jnp.concatenate([half, half], axis=0)
    past = col < row

    def windows(blocks, n_blk):
        chains = [(u, hh) for u in range(len(blocks)) for hh in heads]
        q0s = [_aligned(i * T, T) for i in blocks]
        k0s = [_aligned((i - (n_blk - 1)) * T, T) for i in blocks]
        zs = []
        for u, hh in chains:
            q = q_ref[0, hh, pl.ds(q0s[u], T), :]
            zs.append(lax.dot_general(q, k_ref[0, hh, pl.ds(k0s[u], n_blk * T), :], NT_DIMS,
                                      preferred_element_type=F32))
        stats = []
        for z in zs:
            per_tile = []
            for jb in reversed(range(n_blk)):
                mask = past if jb == n_blk - 1 else None
                per_tile.append(_sb_tile(z[:, jb * T:(jb + 1) * T], neg_tri2, mask))
            stats.append(per_tile)
        out = [([], []) for _ in blocks]
        for (u, hh), per_tile in zip(chains, stats):
            carry = jnp.zeros((T, T), F32)
            ws = []
            for n, (sneg, later, total) in enumerate(per_tile):
                w = jnp.exp((later + carry) - sneg)
                if n == 0:
                    w = jnp.where(past, w, 0.0)
                ws.append(w.astype(BF16))
                carry = carry + total
            w_all = ws[0] if n_blk == 1 else jnp.concatenate(ws[::-1], axis=1)
            out[u][0].append(jnp.dot(w_all, v_ref[0, hh, pl.ds(k0s[u], n_blk * T), :],
                                     preferred_element_type=F32))
            out[u][1].append(carry)
        return out

    def finish(i, n_done, accs, carries):
        q0 = _aligned(i * T, T)

        def largest(cs):
            m = cs[0]
            for c in cs[1:]:
                m = jnp.maximum(m, c)
            return jnp.max(m)

        def cond(st):
            jb, _, _, cmax = st
            return jnp.logical_and(jb >= 0, cmax > SB_UNDERFLOW)

        def body(st):
            jb, accs, carries, _ = st
            k0 = pl.multiple_of(jb * T, T)
            accs, carries = list(accs), list(carries)
            for n, hh in enumerate(heads):
                q = q_ref[0, hh, pl.ds(q0, T), :]
                z = lax.dot_general(q, k_ref[0, hh, pl.ds(k0, T), :], NT_DIMS,
                                    preferred_element_type=F32)
                sneg, later, total = _sb_tile(z, neg_tri2, None)
                w = jnp.exp((later + carries[n]) - sneg).astype(BF16)
                accs[n] = accs[n] + jnp.dot(w, v_ref[0, hh, pl.ds(k0, T), :],
                                            preferred_element_type=F32)
                carries[n] = carries[n] + total
            return jb - 1, tuple(accs), tuple(carries), largest(carries)

        _, accs, _, _ = lax.while_loop(
            cond, body,
            (jnp.int32(i - n_done), tuple(accs), tuple(carries), largest(carries)))
        for n, hh in enumerate(heads):
            o_ref[0, pl.ds(q0, T), hh * SB_DIM:(hh + 1) * SB_DIM] = accs[n].astype(BF16)

    n_win = min(SB_WINDOW, nq)
    n_lead = (n_win - 1) + (nq - (n_win - 1)) % SB_QB
    for i in range(n_lead):
        n_blk = min(i + 1, n_win)
        finish(i, n_blk, *windows([i], n_blk)[0])

    def q_body(p, _):
        blocks = [n_lead + p * SB_QB + u for u in range(SB_QB)]
        for i, (accs, carries) in zip(blocks, windows(blocks, n_win)):
            finish(i, n_win, accs, carries)
        return 0

    lax.fori_loop(0, (nq - n_lead) // SB_QB, q_body, 0)


def _sb_attention(q, k, v):
    b, _, s, _ = q.shape
    hp = LANES // SB_DIM
    spec = pl.BlockSpec((1, hp, s, SB_DIM), lambda bi, h: (bi, h, 0, 0))
    return pl.pallas_call(
        _sb_kernel,
        grid=(b, SB_HEADS // hp),
        in_specs=[spec, spec, spec],
        out_specs=pl.BlockSpec((1, s, LANES), lambda bi, h: (bi, 0, h)),
        out_shape=jax.ShapeDtypeStruct((b, s, SB_WIDTH), BF16),
        compiler_params=pltpu.CompilerParams(
            dimension_semantics=("parallel", "parallel"), vmem_limit_bytes=VMEM_LIMIT),
        name="sb_attention",
    )(q, k, v)


def _rows_bcast(a, offset):
    n_sub = a.shape[0] // GLA_SUB
    return jnp.concatenate(
        [jnp.broadcast_to(a[i * GLA_SUB + offset:i * GLA_SUB + offset + 1, :], (GLA_SUB, a.shape[1]))
         for i in range(n_sub)], axis=0)


def _gla_kernel(q_ref, k_ref, v_ref, go_ref, gl_ref, wup_ref, bup_ref, gn_ref, o_ref, st_ref):
    s_len = q_ref.shape[1]
    C = GLA_CHUNK
    n_sub = C // GLA_SUB
    scale = GLA_DK ** -0.5

    r_cc = lax.broadcasted_iota(jnp.int32, (C, C), 0)
    c_cc = lax.broadcasted_iota(jnp.int32, (C, C), 1)
    ltri = (c_cc <= r_cc).astype(BF16)
    ltri3 = jnp.concatenate([ltri, ltri, ltri], axis=1)
    same_sub = (r_cc // GLA_SUB) == (c_cc // GLA_SUB)
    r_ck = lax.broadcasted_iota(jnp.int32, (C, GLA_DK), 0)
    r_loc = r_ck % GLA_SUB
    sel_r = lax.broadcasted_iota(jnp.int32, (GLA_SUB * GLA_DK, C), 0)
    sel_c = lax.broadcasted_iota(jnp.int32, (GLA_SUB * GLA_DK, C), 1)
    sel = ((sel_r // GLA_DK) == (sel_c % GLA_SUB)).astype(BF16)

    st_ref[...] = jnp.zeros_like(st_ref)

    G = GLA_GROUP

    def group(gi, _):
        t0s = [pl.multiple_of((gi * G + u) * C, C) for u in range(G)]
        qs = [q_ref[0, pl.ds(t0, C), :].astype(F32) * scale for t0 in t0s]
        ks = [k_ref[0, pl.ds(t0, C), :].astype(F32) for t0 in t0s]
        vs = [v_ref[0, pl.ds(t0, C), :] for t0 in t0s]

        xs = [jnp.dot(gl_ref[0, pl.ds(t0, C), :], wup_ref[...], preferred_element_type=F32)
              + bup_ref[...] for t0 in t0s]
        splits = []
        for x in xs:
            log_a = _log_sigmoid(x) * (1.0 / GLA_GATE_TAU)
            hi = log_a.astype(BF16)
            r1 = log_a - hi.astype(F32)
            mid = r1.astype(BF16)
            lo = (r1 - mid.astype(F32)).astype(BF16)
            splits.append(jnp.concatenate([hi, mid, lo], axis=0))
        bcums = [jnp.dot(ltri3, sp, preferred_element_type=F32) for sp in splits]

        offs = []
        for q, k, bcum in zip(qs, ks, bcums):
            qn = (q * jnp.exp(bcum - _rows_bcast(bcum, 0))).astype(BF16)
            off = [jnp.zeros((GLA_SUB, C), F32)]
            for i in range(1, n_sub):
                n_prev = i * GLA_SUB
                ref_row = bcum[n_prev:n_prev + 1, :]
                kn = jnp.concatenate(
                    [(k[:n_prev] * jnp.exp(ref_row - bcum[:n_prev])).astype(BF16),
                     jnp.zeros((C - n_prev, GLA_DK), BF16)], axis=0)
                off.append(lax.dot_general(qn[i * GLA_SUB:(i + 1) * GLA_SUB], kn, NT_DIMS,
                                           preferred_element_type=F32))
            offs.append(jnp.concatenate(off, axis=0))
        scores = []
        for q, k, bcum, off in zip(qs, ks, bcums, offs):
            prods = []
            for s in range(GLA_SUB):
                decay = jnp.exp(jnp.where(r_loc >= s, bcum - _rows_bcast(bcum, s), -jnp.inf))
                prods.append((q * decay * _rows_bcast(k, s)).astype(BF16))
            diag = jnp.dot(jnp.concatenate(prods, axis=1), sel, preferred_element_type=F32)
            scores.append(jnp.where(same_sub, diag, off).astype(BF16))

        kvs, q_ins, decays = [], [], []
        for q, k, v, bcum in zip(qs, ks, vs, bcums):
            b_last = bcum[C - 1:C, :]
            k_dec = (k * jnp.exp(b_last - bcum)).astype(BF16)
            kvs.append(lax.dot_general(v, k_dec, TN_DIMS, preferred_element_type=F32))
            q_ins.append((q * jnp.exp(bcum)).astype(BF16))
            decays.append(jnp.exp(b_last))
        st = st_ref[...]
        outs = []
        for q_in, kv, decay in zip(q_ins, kvs, decays):
            outs.append(lax.dot_general(q_in, st.astype(BF16), NT_DIMS, preferred_element_type=F32))
            st = st * decay + kv
        st_ref[...] = st

        for t0, o, sc, v in zip(t0s, outs, scores, vs):
            o = o + jnp.dot(sc, v, preferred_element_type=F32)
            y = _rms(o, gn_ref[...])
            g = go_ref[0, pl.ds(t0, C), :].astype(F32)
            o_ref[0, pl.ds(t0, C), :] = (y * (g * jax.nn.sigmoid(g))).astype(BF16)
        return 0

    lax.fori_loop(0, s_len // (C * G), group, 0)


def _gla(gqk, gv, gout, glow, wup, bup, gnorm):
    b, s, _ = gv.shape
    return pl.pallas_call(
        _gla_kernel,
        grid=(b, GLA_HEADS),
        in_specs=[
            pl.BlockSpec((1, s, GLA_DK), lambda bi, h: (bi, 0, h)),
            pl.BlockSpec((1, s, GLA_DK), lambda bi, h: (bi, 0, GLA_HEADS + h)),
            pl.BlockSpec((1, s, GLA_DV), lambda bi, h: (bi, 0, h)),
            pl.BlockSpec((1, s, GLA_DV), lambda bi, h: (bi, 0, h)),
            pl.BlockSpec((1, s, LANES), lambda bi, h: (bi, 0, 0)),
            pl.BlockSpec((LANES, GLA_DK), lambda bi, h: (0, h)),
            pl.BlockSpec((1, GLA_DK), lambda bi, h: (0, h)),
            pl.BlockSpec((1, GLA_DV), lambda bi, h: (0, 0)),
        ],
        out_specs=pl.BlockSpec((1, s, GLA_DV), lambda bi, h: (bi, 0, h)),
        out_shape=jax.ShapeDtypeStruct((b, s, GLA_VALUE_WIDTH), BF16),
        scratch_shapes=[pltpu.VMEM((GLA_DV, GLA_DK), F32)],
        compiler_params=pltpu.CompilerParams(
            dimension_semantics=("parallel", "parallel"), vmem_limit_bytes=VMEM_LIMIT),
        name="gla",
    )(gqk, gqk, gv, gout, glow, wup, bup, gnorm)


def _merge_kernel(x_ref, osb_ref, ogla_ref, gates_ref, wpa_ref, wpb_ref, wout_ref, o_ref):
    d = x_ref.shape[-1]
    a = jnp.dot(osb_ref[0], wpa_ref[...], preferred_element_type=F32)
    c = jnp.dot(ogla_ref[0], wpb_ref[...], preferred_element_type=F32)
    ga = jax.nn.sigmoid(gates_ref[0, :, 0:d].astype(F32))
    gb = jax.nn.sigmoid(gates_ref[0, :, d:2 * d].astype(F32))
    mix = ga * a + gb * c
    o_ref[0] = x_ref[0] + jnp.dot(mix.astype(BF16), wout_ref[...], preferred_element_type=F32)


def _merge(x, osb, ogla, gates, wpa, wpb, wout, tm):
    b, s, d = x.shape
    tok = lambda width: pl.BlockSpec((1, tm, width), lambda bi, i: (bi, i, 0))
    full = lambda w: pl.BlockSpec(w.shape, lambda bi, i: (0, 0), pipeline_mode=pl.Buffered(1))
    return pl.pallas_call(
        _merge_kernel,
        grid=(b, s // tm),
        in_specs=[tok(d), tok(SB_WIDTH), tok(GLA_VALUE_WIDTH), tok(2 * d), full(wpa), full(wpb), full(wout)],
        out_specs=tok(d),
        out_shape=jax.ShapeDtypeStruct((b, s, d), F32),
        compiler_params=pltpu.CompilerParams(
            dimension_semantics=("parallel", "parallel"), vmem_limit_bytes=VMEM_LIMIT),
        name="merge",
    )(x, osb, ogla, gates, wpa, wpb, wout)


def _mlp_kernel(x_ref, g_ref, w1_ref, w2_ref, gf_ref, o_ref, *, ff_chunk):
    x = x_ref[0]
    h = _rms(x, g_ref[...]).astype(BF16)
    d_ff = w1_ref.shape[1]
    acc = jnp.zeros_like(x)
    for c in range(d_ff // ff_chunk):
        a = jnp.dot(h, w1_ref[:, c * ff_chunk:(c + 1) * ff_chunk], preferred_element_type=F32)
        a = jnp.square(jnp.maximum(a, 0.0)).astype(BF16)
        acc = acc + jnp.dot(a, w2_ref[c * ff_chunk:(c + 1) * ff_chunk, :], preferred_element_type=F32)
    o_ref[0] = _rms(x + acc, gf_ref[...])


def _mlp(x, g, w1, w2, gf, tm, ff_chunk):
    b, s, d = x.shape
    tok = pl.BlockSpec((1, tm, d), lambda bi, i: (bi, i, 0))
    vec = pl.BlockSpec((1, d), lambda bi, i: (0, 0))
    full = lambda w: pl.BlockSpec(w.shape, lambda bi, i: (0, 0), pipeline_mode=pl.Buffered(1))
    return pl.pallas_call(
        functools.partial(_mlp_kernel, ff_chunk=ff_chunk),
        grid=(b, s // tm),
        in_specs=[tok, vec, full(w1), full(w2), vec],
        out_specs=tok,
        out_shape=jax.ShapeDtypeStruct((b, s, d), F32),
        compiler_params=pltpu.CompilerParams(
            dimension_semantics=("parallel", "parallel"), vmem_limit_bytes=VMEM_LIMIT),
        name="mlp",
    )(x, g, w1, w2, gf)


def _layer(x, norm_mix, w_in, w_gate_up, b_gate_up, gla_norm, w_proj_sb, w_proj_gla, w_out,
           norm_mlp, w_ff1, w_ff2, final_gain, tm):
    d = x.shape[-1]
    c_low = 3 * SB_WIDTH + 2 * GLA_KEY_WIDTH + 2 * GLA_VALUE_WIDTH
    w_main = jnp.concatenate(
        [w_in[:, :c_low], w_in[:, c_low + GLA_GATE_RANK:],
         w_in[:, c_low:c_low + GLA_GATE_RANK],
         jnp.zeros((d, LANES - GLA_GATE_RANK), w_in.dtype)], axis=1).astype(BF16)
    wup = jnp.concatenate(
        [w_gate_up, jnp.zeros((LANES - GLA_GATE_RANK, GLA_KEY_WIDTH), w_gate_up.dtype)], axis=0).astype(BF16)

    sbq, sbk, sbv, gqk, gv, gout, gates, glow = _inproj(x, norm_mix.reshape(1, d), w_main, tm)
    o_sb = _sb_attention(sbq, sbk, sbv)
    o_gla = _gla(gqk, gv, gout, glow, wup, b_gate_up.reshape(1, -1), gla_norm.reshape(1, -1))
    x = _merge(x, o_sb, o_gla, gates, w_proj_sb.astype(BF16), w_proj_gla.astype(BF16),
               w_out.astype(BF16), tm)
    return _mlp(x, norm_mlp.reshape(1, d), w_ff1.astype(BF16), w_ff2.astype(BF16), final_gain, tm,
                ff_chunk=1024)


def kernel(x, norm_mix, w_in, w_gate_up, b_gate_up, gla_norm, w_proj_sb, w_proj_gla, w_out,
           norm_mlp, w_ff1, w_ff2, norm_final):
    depth = w_in.shape[0]
    s = x.shape[1]
    tm = min(256, s)
    assert depth == 1, "the final RMSNorm is fused into the last layer's MLP kernel"
    return _layer(x, norm_mix[0], w_in[0], w_gate_up[0], b_gate_up[0], gla_norm[0], w_proj_sb[0],
                  w_proj_gla[0], w_out[0], norm_mlp[0], w_ff1[0], w_ff2[0],
                  norm_final.reshape(1, -1), tm)
```

```python
import functools

import jax
import jax.numpy as jnp
from jax import lax
from jax.experimental import pallas as pl
from jax.experimental.pallas import tpu as pltpu

F32 = jnp.float32
BF16 = jnp.bfloat16
EPS = 1e-6

SB_HEADS = 8
SB_DIM = 64
SB_WIDTH = SB_HEADS * SB_DIM
GLA_HEADS = 4
GLA_DK = 128
GLA_DV = 256
GLA_KEY_WIDTH = GLA_HEADS * GLA_DK
GLA_VALUE_WIDTH = GLA_HEADS * GLA_DV
GLA_GATE_RANK = 16
GLA_GATE_TAU = 16.0
GLA_CHUNK = 64
GLA_SUB = 16
GLA_GROUP = 8
GLA_GATE_SPAN = 2
GLA_SAFE_SPREAD = 60.0
LANES = 128

SB_TQ = 128
SB_WINDOW = 3
SB_QB = 4
SB_UNDERFLOW = -110.0

VMEM_LIMIT = 56 * 1024 * 1024

NT_DIMS = (((1,), (1,)), ((), ()))
TN_DIMS = (((0,), (0,)), ((), ()))


def _rms(x, g):
    ms = jnp.mean(x * x, axis=-1, keepdims=True)
    return (x * lax.rsqrt(ms + EPS)) * g


def _log_sigmoid(x):
    return -(jnp.maximum(-x, 0.0) + jnp.log(1.0 + jnp.exp(-jnp.abs(x))))


def _inproj_kernel(x_ref, g_ref, w_ref, sbq_ref, sbk_ref, sbv_ref,
                   gqk_ref, gv_ref, gout_ref, gates_ref, glow_ref):
    h = _rms(x_ref[0], g_ref[...]).astype(BF16)

    def proj(c0, width):
        return jnp.dot(h, w_ref[:, c0:c0 + width], preferred_element_type=F32)

    c = 0
    for idx, ref in enumerate((sbq_ref, sbk_ref, sbv_ref)):
        r = proj(c, SB_WIDTH)
        if idx == 0:
            r = r * (SB_DIM ** -0.5)
        r = r.astype(BF16)
        for hd in range(SB_HEADS):
            ref[0, hd] = r[:, hd * SB_DIM:(hd + 1) * SB_DIM]
        c += SB_WIDTH
    for ref, width in ((gqk_ref, 2 * GLA_KEY_WIDTH), (gv_ref, GLA_VALUE_WIDTH),
                       (gout_ref, GLA_VALUE_WIDTH)):
        ref[0] = proj(c, width).astype(BF16)
        c += width
    d_model = x_ref.shape[-1]
    for half in range(2):
        gates_ref[0, :, half * d_model:(half + 1) * d_model] = proj(c, d_model).astype(BF16)
        c += d_model
    glow_ref[0] = proj(c, LANES).astype(BF16)


def _inproj(x, norm_g, w_all, tm):
    b, s, d = x.shape
    n_all = w_all.shape[1]
    grid = (b, s // tm)
    tok = lambda width: pl.BlockSpec((1, tm, width), lambda bi, i: (bi, i, 0))
    head = pl.BlockSpec((1, SB_HEADS, tm, SB_DIM), lambda bi, i: (bi, 0, i, 0))
    out_shape = (
        jax.ShapeDtypeStruct((b, SB_HEADS, s, SB_DIM), BF16),
        jax.ShapeDtypeStruct((b, SB_HEADS, s, SB_DIM), BF16),
        jax.ShapeDtypeStruct((b, SB_HEADS, s, SB_DIM), BF16),
        jax.ShapeDtypeStruct((b, s, 2 * GLA_KEY_WIDTH), BF16),
        jax.ShapeDtypeStruct((b, s, GLA_VALUE_WIDTH), BF16),
        jax.ShapeDtypeStruct((b, s, GLA_VALUE_WIDTH), BF16),
        jax.ShapeDtypeStruct((b, s, 2 * d), BF16),
        jax.ShapeDtypeStruct((b, s, LANES), BF16),
    )
    return pl.pallas_call(
        _inproj_kernel,
        grid=grid,
        in_specs=[
            tok(d),
            pl.BlockSpec((1, d), lambda bi, i: (0, 0)),
            pl.BlockSpec((d, n_all), lambda bi, i: (0, 0), pipeline_mode=pl.Buffered(1)),
        ],
        out_specs=(head, head, head, tok(2 * GLA_KEY_WIDTH), tok(GLA_VALUE_WIDTH),
                   tok(GLA_VALUE_WIDTH), tok(2 * d), tok(LANES)),
        out_shape=out_shape,
        compiler_params=pltpu.CompilerParams(
            dimension_semantics=("parallel", "parallel"), vmem_limit_bytes=VMEM_LIMIT),
        name="inproj",
    )(x, norm_g, w_all)


def _aligned(start, tile):
    return start if isinstance(start, int) else pl.multiple_of(start, tile)


def _sb_tile(z, neg_tri2, mask):
    tk = z.shape[1]
    f = jnp.maximum(z, 0.0) + jnp.log(1.0 + jnp.exp(-jnp.abs(z)))
    sneg = f - z
    if mask is not None:
        f = jnp.where(mask, f, 0.0)
    hi = f.astype(BF16)
    lo = (f - hi.astype(F32)).astype(BF16)
    cs = jnp.dot(jnp.concatenate([hi, lo], axis=1), neg_tri2, preferred_element_type=F32)
    return sneg, cs[:, :tk], cs[:, tk:]


def _sb_kernel(q_ref, k_ref, v_ref, o_ref):
    s_len = q_ref.shape[2]
    heads = range(q_ref.shape[1])
    T = SB_TQ
    nq = s_len // T
    row = lax.broadcasted_iota(jnp.int32, (T, T), 0)
    col = lax.broadcasted_iota(jnp.int32, (T, T), 1)
    neg_tri = jnp.where(row > col, -1.0, 0.0).astype(BF16)
    half = jnp.concatenate([neg_tri, jnp.full((T, T), -1.0, BF16)], axis=1)
    neg_tri2 = jnp.concatenate([half, half], axis=0)
    past = col < row

    def windows(blocks, n_blk):
        chains = [(u, hh) for u in range(len(blocks)) for hh in heads]
        q0s = [_aligned(i * T, T) for i in blocks]
        k0s = [_aligned((i - (n_blk - 1)) * T, T) for i in blocks]
        zs = []
        for u, hh in chains:
            q = q_ref[0, hh, pl.ds(q0s[u], T), :]
            zs.append(lax.dot_general(q, k_ref[0, hh, pl.ds(k0s[u], n_blk * T), :], NT_DIMS,
                                      preferred_element_type=F32))
        stats = []
        for z in zs:
            per_tile = []
            for jb in reversed(range(n_blk)):
                mask = past if jb == n_blk - 1 else None
                per_tile.append(_sb_tile(z[:, jb * T:(jb + 1) * T], neg_tri2, mask))
            stats.append(per_tile)
        out = [([], []) for _ in blocks]
        for (u, hh), per_tile in zip(chains, stats):
            carry = jnp.zeros((T, T), F32)
            ws = []
            for n, (sneg, later, total) in enumerate(per_tile):
                w = jnp.exp((later + carry) - sneg)
                if n == 0:
                    w = jnp.where(past, w, 0.0)
                ws.append(w.astype(BF16))
                carry = carry + total
            w_all = ws[0] if n_blk == 1 else jnp.concatenate(ws[::-1], axis=1)
            out[u][0].append(jnp.dot(w_all, v_ref[0, hh, pl.ds(k0s[u], n_blk * T), :],
                                     preferred_element_type=F32))
            out[u][1].append(carry)
        return out

    def finish(i, n_done, accs, carries):
        q0 = _aligned(i * T, T)

        def largest(cs):
            m = cs[0]
            for c in cs[1:]:
                m = jnp.maximum(m, c)
            return jnp.max(m)

        def cond(st):
            jb, _, _, cmax = st
            return jnp.logical_and(jb >= 0, cmax > SB_UNDERFLOW)

        def body(st):
            jb, accs, carries, _ = st
            k0 = pl.multiple_of(jb * T, T)
            accs, carries = list(accs), list(carries)
            for n, hh in enumerate(heads):
                q = q_ref[0, hh, pl.ds(q0, T), :]
                z = lax.dot_general(q, k_ref[0, hh, pl.ds(k0, T), :], NT_DIMS,
                                    preferred_element_type=F32)
                sneg, later, total = _sb_tile(z, neg_tri2, None)
                w = jnp.exp((later + carries[n]) - sneg).astype(BF16)
                accs[n] = accs[n] + jnp.dot(w, v_ref[0, hh, pl.ds(k0, T), :],
                                            preferred_element_type=F32)
                carries[n] = carries[n] + total
            return jb - 1, tuple(accs), tuple(carries), largest(carries)

        _, accs, _, _ = lax.while_loop(
            cond, body,
            (jnp.int32(i - n_done), tuple(accs), tuple(carries), largest(carries)))
        for n, hh in enumerate(heads):
            o_ref[0, pl.ds(q0, T), hh * SB_DIM:(hh + 1) * SB_DIM] = accs[n].astype(BF16)

    n_win = min(SB_WINDOW, nq)
    n_lead = (n_win - 1) + (nq - (n_win - 1)) % SB_QB
    for i in range(n_lead):
        n_blk = min(i + 1, n_win)
        finish(i, n_blk, *windows([i], n_blk)[0])

    def q_body(p, _):
        blocks = [n_lead + p * SB_QB + u for u in range(SB_QB)]
        for i, (accs, carries) in zip(blocks, windows(blocks, n_win)):
            finish(i, n_win, accs, carries)
        return 0

    lax.fori_loop(0, (nq - n_lead) // SB_QB, q_body, 0)


def _sb_attention(q, k, v):
    b, _, s, _ = q.shape
    hp = LANES // SB_DIM
    spec = pl.BlockSpec((1, hp, s, SB_DIM), lambda bi, h: (bi, h, 0, 0))
    return pl.pallas_call(
        _sb_kernel,
        grid=(b, SB_HEADS // hp),
        in_specs=[spec, spec, spec],
        out_specs=pl.BlockSpec((1, s, LANES), lambda bi, h: (bi, 0, h)),
        out_shape=jax.ShapeDtypeStruct((b, s, SB_WIDTH), BF16),
        compiler_params=pltpu.CompilerParams(
            dimension_semantics=("parallel", "parallel"), vmem_limit_bytes=VMEM_LIMIT),
        name="sb_attention",
    )(q, k, v)


def _rows_bcast(a, offset):
    n_sub = a.shape[0] // GLA_SUB
    return jnp.concatenate(
        [jnp.broadcast_to(a[i * GLA_SUB + offset:i * GLA_SUB + offset + 1, :], (GLA_SUB, a.shape[1]))
         for i in range(n_sub)], axis=0)


def _gla_kernel(q_ref, k_ref, v_ref, go_ref, gl_ref, wup_ref, bup_ref, gn_ref, o_ref,
                st_ref, bc_ref, safe_ref):
    s_len = q_ref.shape[1]
    C = GLA_CHUNK
    G = GLA_GROUP
    n_sub = C // GLA_SUB
    scale = GLA_DK ** -0.5

    r_cc = lax.broadcasted_iota(jnp.int32, (C, C), 0)
    c_cc = lax.broadcasted_iota(jnp.int32, (C, C), 1)
    causal = c_cc <= r_cc
    ltri = causal.astype(BF16)
    ltri3 = jnp.concatenate([ltri, ltri, ltri], axis=1)
    diag_keep = jnp.logical_and((r_cc // GLA_SUB) == (c_cc // GLA_SUB), causal)
    r_loc = lax.broadcasted_iota(jnp.int32, (C, GLA_DK), 0) % GLA_SUB
    row_pick = [(r_loc == s).astype(BF16) for s in range(GLA_SUB)]

    n_groups = s_len // (C * G)
    gate_span = min(GLA_GATE_SPAN, n_groups)

    def gates(pi, _):
        t0s = [pl.multiple_of((pi * gate_span * G + u) * C, C) for u in range(gate_span * G)]
        xs = [jnp.dot(gl_ref[0, pl.ds(t0, C), :], wup_ref[...], preferred_element_type=F32)
              + bup_ref[...] for t0 in t0s]
        splits = []
        for x in xs:
            log_a = _log_sigmoid(x) * (1.0 / GLA_GATE_TAU)
            hi = log_a.astype(BF16)
            r1 = log_a - hi.astype(F32)
            mid = r1.astype(BF16)
            lo = (r1 - mid.astype(F32)).astype(BF16)
            splits.append(jnp.concatenate([hi, mid, lo], axis=0))
        spreads = []
        for t0, sp in zip(t0s, splits):
            bcum = jnp.dot(ltri3, sp, preferred_element_type=F32)
            bc_ref[pl.ds(t0, C), :] = bcum
            spreads.append(bcum[0:1, :] - bcum[C - 1:C, :])
        for g in range(gate_span):
            worst = functools.reduce(jnp.maximum, spreads[g * G:(g + 1) * G])
            safe_ref[pi * gate_span + g] = (jnp.max(worst) < GLA_SAFE_SPREAD).astype(jnp.int32)
        return 0

    def scores_factored(qs, ks, bcums):
        out = []
        for q, k, bcum in zip(qs, ks, bcums):
            r0 = bcum[0:1, :]
            qn = (q * jnp.exp(bcum - r0)).astype(BF16)
            kn = (k * jnp.exp(r0 - bcum)).astype(BF16)
            sc = lax.dot_general(qn, kn, NT_DIMS, preferred_element_type=F32)
            out.append(jnp.where(causal, sc, 0.0).astype(BF16))
        return out

    def scores_bounded(qs, ks, t0s, bcums):
        offs = []
        for q, k, bcum in zip(qs, ks, bcums):
            qn = (q * jnp.exp(bcum - _rows_bcast(bcum, 0))).astype(BF16)
            off = [jnp.zeros((GLA_SUB, C), F32)]
            for i in range(1, n_sub):
                n_prev = i * GLA_SUB
                ref_row = bcum[n_prev:n_prev + 1, :]
                kn = jnp.concatenate(
                    [(k[:n_prev] * jnp.exp(ref_row - bcum[:n_prev])).astype(BF16),
                     jnp.zeros((C - n_prev, GLA_DK), BF16)], axis=0)
                off.append(lax.dot_general(qn[i * GLA_SUB:(i + 1) * GLA_SUB], kn, NT_DIMS,
                                           preferred_element_type=F32))
            offs.append(jnp.concatenate(off, axis=0))
        out = []
        for q, t0, bcum, off in zip(qs, t0s, bcums, offs):
            kb = k_ref[0, pl.ds(t0, C), :]
            prods, keys = [], []
            for s in range(GLA_SUB):
                decay = jnp.exp(jnp.minimum(bcum - _rows_bcast(bcum, s), 0.0))
                prods.append((q * decay).astype(BF16))
                keys.append(kb * row_pick[s])
            diag = lax.dot_general(jnp.concatenate(prods, axis=1), jnp.concatenate(keys, axis=1),
                                   NT_DIMS, preferred_element_type=F32)
            out.append(jnp.where(diag_keep, diag, off).astype(BF16))
        return out

    def group(gi, factored):
        t0s = [pl.multiple_of((gi * G + u) * C, C) for u in range(G)]
        qs = [q_ref[0, pl.ds(t0, C), :].astype(F32) * scale for t0 in t0s]
        ks = [k_ref[0, pl.ds(t0, C), :].astype(F32) for t0 in t0s]
        vs = [v_ref[0, pl.ds(t0, C), :] for t0 in t0s]
        bcums = [bc_ref[pl.ds(t0, C), :] for t0 in t0s]
        scores = scores_factored(qs, ks, bcums) if factored else scores_bounded(qs, ks, t0s, bcums)

        kvs, q_ins, decays = [], [], []
        for q, k, v, bcum in zip(qs, ks, vs, bcums):
            b_last = bcum[C - 1:C, :]
            k_dec = (k * jnp.exp(b_last - bcum)).astype(BF16)
            kvs.append(lax.dot_general(v, k_dec, TN_DIMS, preferred_element_type=F32))
            q_ins.append((q * jnp.exp(bcum)).astype(BF16))
            decays.append(jnp.exp(b_last))
        st = st_ref[...]
        outs = []
        for q_in, kv, decay in zip(q_ins, kvs, decays):
            outs.append(lax.dot_general(q_in, st.astype(BF16), NT_DIMS, preferred_element_type=F32))
            st = st * decay + kv
        st_ref[...] = st

        for t0, o, sc, v in zip(t0s, outs, scores, vs):
            o = o + jnp.dot(sc, v, preferred_element_type=F32)
            y = _rms(o, gn_ref[...])
            g = go_ref[0, pl.ds(t0, C), :].astype(F32)
            o_ref[0, pl.ds(t0, C), :] = (y * (g * jax.nn.sigmoid(g))).astype(BF16)

    def main(gi, _):
        lax.cond(safe_ref[gi] == 1, lambda: group(gi, True), lambda: group(gi, False))
        return 0

    lax.fori_loop(0, n_groups // gate_span, gates, 0)
    st_ref[...] = jnp.zeros_like(st_ref)
    lax.fori_loop(0, n_groups, main, 0)


def _gla(gqk, gv, gout, glow, wup, bup, gnorm):
    b, s, _ = gv.shape
    n_groups, rem = divmod(s, GLA_CHUNK * GLA_GROUP)
    assert rem == 0 and n_groups % min(GLA_GATE_SPAN, n_groups) == 0, s
    return pl.pallas_call(
        _gla_kernel,
        grid=(b, GLA_HEADS),
        in_specs=[
            pl.BlockSpec((1, s, GLA_DK), lambda bi, h: (bi, 0, h)),
            pl.BlockSpec((1, s, GLA_DK), lambda bi, h: (bi, 0, GLA_HEADS + h)),
            pl.BlockSpec((1, s, GLA_DV), lambda bi, h: (bi, 0, h)),
            pl.BlockSpec((1, s, GLA_DV), lambda bi, h: (bi, 0, h)),
            pl.BlockSpec((1, s, LANES), lambda bi, h: (bi, 0, 0)),
            pl.BlockSpec((LANES, GLA_DK), lambda bi, h: (0, h)),
            pl.BlockSpec((1, GLA_DK), lambda bi, h: (0, h)),
            pl.BlockSpec((1, GLA_DV), lambda bi, h: (0, 0)),
        ],
        out_specs=pl.BlockSpec((1, s, GLA_DV), lambda bi, h: (bi, 0, h)),
        out_shape=jax.ShapeDtypeStruct((b, s, GLA_VALUE_WIDTH), BF16),
        scratch_shapes=[pltpu.VMEM((GLA_DV, GLA_DK), F32),
                        pltpu.VMEM((s, GLA_DK), F32),
                        pltpu.SMEM((s // (GLA_CHUNK * GLA_GROUP),), jnp.int32)],
        compiler_params=pltpu.CompilerParams(
            dimension_semantics=("parallel", "parallel"), vmem_limit_bytes=VMEM_LIMIT),
        name="gla",
    )(gqk, gqk, gv, gout, glow, wup, bup, gnorm)


def _merge_kernel(x_ref, osb_ref, ogla_ref, gates_ref, wpa_ref, wpb_ref, wout_ref, o_ref):
    d = x_ref.shape[-1]
    a = jnp.dot(osb_ref[0], wpa_ref[...], preferred_element_type=F32)
    c = jnp.dot(ogla_ref[0], wpb_ref[...], preferred_element_type=F32)
    ga = jax.nn.sigmoid(gates_ref[0, :, 0:d].astype(F32))
    gb = jax.nn.sigmoid(gates_ref[0, :, d:2 * d].astype(F32))
    mix = ga * a + gb * c
    o_ref[0] = x_ref[0] + jnp.dot(mix.astype(BF16), wout_ref[...], preferred_element_type=F32)


def _merge(x, osb, ogla, gates, wpa, wpb, wout, tm):
    b, s, d = x.shape
    tok = lambda width: pl.BlockSpec((1, tm, width), lambda bi, i: (bi, i, 0))
    full = lambda w: pl.BlockSpec(w.shape, lambda bi, i: (0, 0), pipeline_mode=pl.Buffered(1))
    return pl.pallas_call(
        _merge_kernel,
        grid=(b, s // tm),
        in_specs=[tok(d), tok(SB_WIDTH), tok(GLA_VALUE_WIDTH), tok(2 * d), full(wpa), full(wpb), full(wout)],
        out_specs=tok(d),
        out_shape=jax.ShapeDtypeStruct((b, s, d), F32),
        compiler_params=pltpu.CompilerParams(
            dimension_semantics=("parallel", "parallel"), vmem_limit_bytes=VMEM_LIMIT),
        name="merge",
    )(x, osb, ogla, gates, wpa, wpb, wout)


def _mlp_kernel(x_ref, g_ref, w1_ref, w2_ref, gf_ref, o_ref, *, ff_chunk):
    x = x_ref[0]
    h = _rms(x, g_ref[...]).astype(BF16)
    d_ff = w1_ref.shape[1]
    acc = jnp.zeros_like(x)
    for c in range(d_ff // ff_chunk):
        a = jnp.dot(h, w1_ref[:, c * ff_chunk:(c + 1) * ff_chunk], preferred_element_type=F32)
        a = jnp.square(jnp.maximum(a, 0.0)).astype(BF16)
        acc = acc + jnp.dot(a, w2_ref[c * ff_chunk:(c + 1) * ff_chunk, :], preferred_element_type=F32)
    o_ref[0] = _rms(x + acc, gf_ref[...])


def _mlp(x, g, w1, w2, gf, tm, ff_chunk):
    b, s, d = x.shape
    tok = pl.BlockSpec((1, tm, d), lambda bi, i: (bi, i, 0))
    vec = pl.BlockSpec((1, d), lambda bi, i: (0, 0))
    full = lambda w: pl.BlockSpec(w.shape, lambda bi, i: (0, 0), pipeline_mode=pl.Buffered(1))
    return pl.pallas_call(
        functools.partial(_mlp_kernel, ff_chunk=ff_chunk),
        grid=(b, s // tm),
        in_specs=[tok, vec, full(w1), full(w2), vec],
        out_specs=tok,
        out_shape=jax.ShapeDtypeStruct((b, s, d), F32),
        compiler_params=pltpu.CompilerParams(
            dimension_semantics=("parallel", "parallel"), vmem_limit_bytes=VMEM_LIMIT),
        name="mlp",
    )(x, g, w1, w2, gf)


def _layer(x, norm_mix, w_in, w_gate_up, b_gate_up, gla_norm, w_proj_sb, w_proj_gla, w_out,
           norm_mlp, w_ff1, w_ff2, final_gain, tm):
    d = x.shape[-1]
    c_low = 3 * SB_WIDTH + 2 * GLA_KEY_WIDTH + 2 * GLA_VALUE_WIDTH
    w_main = jnp.concatenate(
        [w_in[:, :c_low], w_in[:, c_low + GLA_GATE_RANK:],
         w_in[:, c_low:c_low + GLA_GATE_RANK],
         jnp.zeros((d, LANES - GLA_GATE_RANK), w_in.dtype)], axis=1).astype(BF16)
    wup = jnp.concatenate(
        [w_gate_up, jnp.zeros((LANES - GLA_GATE_RANK, GLA_KEY_WIDTH), w_gate_up.dtype)], axis=0).astype(BF16)

    sbq, sbk, sbv, gqk, gv, gout, gates, glow = _inproj(x, norm_mix.reshape(1, d), w_main, tm)
    o_sb = _sb_attention(sbq, sbk, sbv)
    o_gla = _gla(gqk, gv, gout, glow, wup, b_gate_up.reshape(1, -1), gla_norm.reshape(1, -1))
    x = _merge(x, o_sb, o_gla, gates, w_proj_sb.astype(BF16), w_proj_gla.astype(BF16),
               w_out.astype(BF16), tm)
    return _mlp(x, norm_mlp.reshape(1, d), w_ff1.astype(BF16), w_ff2.astype(BF16), final_gain, tm,
                ff_chunk=1024)


def kernel(x, norm_mix, w_in, w_gate_up, b_gate_up, gla_norm, w_proj_sb, w_proj_gla, w_out,
           norm_mlp, w_ff1, w_ff2, norm_final):
    depth = w_in.shape[0]
    s = x.shape[1]
    tm = min(512, s)
    assert depth == 1, "the final RMSNorm is fused into the last layer's MLP kernel"
    return _layer(x, norm_mix[0], w_in[0], w_gate_up[0], b_gate_up[0], gla_norm[0], w_proj_sb[0],
                  w_proj_gla[0], w_out[0], norm_mlp[0], w_ff1[0], w_ff2[0],
                  norm_final.reshape(1, -1), tm)
```

```python
import functools

import jax
import jax.numpy as jnp
from jax import lax
from jax.experimental import pallas as pl
from jax.experimental.pallas import tpu as pltpu

F32 = jnp.float32
BF16 = jnp.bfloat16
EPS = 1e-6

SB_HEADS = 8
SB_DIM = 64
SB_WIDTH = SB_HEADS * SB_DIM
GLA_HEADS = 4
GLA_DK = 128
GLA_DV = 256
GLA_KEY_WIDTH = GLA_HEADS * GLA_DK
GLA_VALUE_WIDTH = GLA_HEADS * GLA_DV
GLA_GATE_RANK = 16
GLA_GATE_TAU = 16.0
GLA_CHUNK = 64
GLA_SUB = 16
GLA_GROUP = 8
GLA_GATE_SPAN = 2
GLA_SAFE_SPREAD = 60.0
LANES = 128

SB_TQ = 128
SB_WINDOW = 3
SB_QB = 4
SB_UNDERFLOW = -110.0

VMEM_LIMIT = 56 * 1024 * 1024

NT_DIMS = (((1,), (1,)), ((), ()))
TN_DIMS = (((0,), (0,)), ((), ()))


def _rms(x, g):
    ms = jnp.mean(x * x, axis=-1, keepdims=True)
    return (x * lax.rsqrt(ms + EPS)) * g


def _log_sigmoid(x):
    return -(jnp.maximum(-x, 0.0) + jnp.log(1.0 + jnp.exp(-jnp.abs(x))))


def _inproj_kernel(x_ref, g_ref, wa_ref, wg_ref, wl_ref, sbq_ref, sbk_ref, sbv_ref,
                   gqk_ref, gv_ref, gout_ref, gates_ref, glow_ref):
    h = _rms(x_ref[0], g_ref[...]).astype(BF16)

    def proj(w_ref, c0, width):
        return jnp.dot(h, w_ref[:, c0:c0 + width], preferred_element_type=F32)

    c = 0
    for idx, ref in enumerate((sbq_ref, sbk_ref, sbv_ref)):
        r = proj(wa_ref, c, SB_WIDTH)
        if idx == 0:
            r = r * (SB_DIM ** -0.5)
        r = r.astype(BF16)
        for hd in range(SB_HEADS):
            ref[0, hd] = r[:, hd * SB_DIM:(hd + 1) * SB_DIM]
        c += SB_WIDTH
    for ref, width in ((gqk_ref, 2 * GLA_KEY_WIDTH), (gv_ref, GLA_VALUE_WIDTH),
                       (gout_ref, GLA_VALUE_WIDTH)):
        ref[0] = proj(wa_ref, c, width).astype(BF16)
        c += width
    d_model = x_ref.shape[-1]
    for half in range(2):
        gates_ref[0, :, half * d_model:(half + 1) * d_model] = (
            proj(wg_ref, half * d_model, d_model).astype(BF16))
    glow_ref[0] = proj(wl_ref, 0, LANES).astype(BF16)


def _inproj(x, norm_g, w_main, w_gates, w_low, tm):
    b, s, d = x.shape
    grid = (b, s // tm)
    tok = lambda width: pl.BlockSpec((1, tm, width), lambda bi, i: (bi, i, 0))
    head = pl.BlockSpec((1, SB_HEADS, tm, SB_DIM), lambda bi, i: (bi, 0, i, 0))
    out_shape = (
        jax.ShapeDtypeStruct((b, SB_HEADS, s, SB_DIM), BF16),
        jax.ShapeDtypeStruct((b, SB_HEADS, s, SB_DIM), BF16),
        jax.ShapeDtypeStruct((b, SB_HEADS, s, SB_DIM), BF16),
        jax.ShapeDtypeStruct((b, s, 2 * GLA_KEY_WIDTH), BF16),
        jax.ShapeDtypeStruct((b, s, GLA_VALUE_WIDTH), BF16),
        jax.ShapeDtypeStruct((b, s, GLA_VALUE_WIDTH), BF16),
        jax.ShapeDtypeStruct((b, s, 2 * d), BF16),
        jax.ShapeDtypeStruct((b, s, LANES), BF16),
    )
    return pl.pallas_call(
        _inproj_kernel,
        grid=grid,
        in_specs=[
            tok(d),
            pl.BlockSpec((1, d), lambda bi, i: (0, 0)),
        ] + [pl.BlockSpec(w.shape, lambda bi, i: (0, 0), pipeline_mode=pl.Buffered(1))
             for w in (w_main, w_gates, w_low)],
        out_specs=(head, head, head, tok(2 * GLA_KEY_WIDTH), tok(GLA_VALUE_WIDTH),
                   tok(GLA_VALUE_WIDTH), tok(2 * d), tok(LANES)),
        out_shape=out_shape,
        compiler_params=pltpu.CompilerParams(
            dimension_semantics=("parallel", "parallel"), vmem_limit_bytes=VMEM_LIMIT),
        name="inproj",
    )(x, norm_g, w_main, w_gates, w_low)


def _aligned(start, tile):
    return start if isinstance(start, int) else pl.multiple_of(start, tile)


def _sb_tile(z, neg_tri2, mask):
    tk = z.shape[1]
    f = jnp.maximum(z, 0.0) + jnp.log(1.0 + jnp.exp(-jnp.abs(z)))
    sneg = f - z
    if mask is not None:
        f = jnp.where(mask, f, 0.0)
    hi = f.astype(BF16)
    lo = (f - hi.astype(F32)).astype(BF16)
    cs = jnp.dot(jnp.concatenate([hi, lo], axis=1), neg_tri2, preferred_element_type=F32)
    return sneg, cs[:, :tk], cs[:, tk:]


def _sb_kernel(q_ref, k_ref, v_ref, o_ref, acc_ref, carry_ref):
    s_len = q_ref.shape[2]
    heads = range(q_ref.shape[1])
    T = SB_TQ
    nq = s_len // T
    row = lax.broadcasted_iota(jnp.int32, (T, T), 0)
    col = lax.broadcasted_iota(jnp.int32, (T, T), 1)
    neg_tri = jnp.where(row > col, -1.0, 0.0).astype(BF16)
    half = jnp.concatenate([neg_tri, jnp.full((T, T), -1.0, BF16)], axis=1)
    neg_tri2 = jnp.concatenate([half, half], axis=0)
    past = col < row

    def windows(blocks, n_blk):
        chains = [(u, hh) for u in range(len(blocks)) for hh in heads]
        q0s = [_aligned(i * T, T) for i in blocks]
        k0s = [_aligned((i - (n_blk - 1)) * T, T) for i in blocks]
        zs = []
        for u, hh in chains:
            q = q_ref[0, hh, pl.ds(q0s[u], T), :]
            zs.append(lax.dot_general(q, k_ref[0, hh, pl.ds(k0s[u], n_blk * T), :], NT_DIMS,
                                      preferred_element_type=F32))
        stats = []
        for z in zs:
            per_tile = []
            for jb in reversed(range(n_blk)):
                mask = past if jb == n_blk - 1 else None
                per_tile.append(_sb_tile(z[:, jb * T:(jb + 1) * T], neg_tri2, mask))
            stats.append(per_tile)
        out = [([], []) for _ in blocks]
        for (u, hh), per_tile in zip(chains, stats):
            carry = jnp.zeros((T, T), F32)
            ws = []
            for n, (sneg, later, total) in enumerate(per_tile):
                w = jnp.exp((later + carry) - sneg)
                if n == 0:
                    w = jnp.where(past, w, 0.0)
                ws.append(w.astype(BF16))
                carry = carry + total
            w_all = ws[0] if n_blk == 1 else jnp.concatenate(ws[::-1], axis=1)
            out[u][0].append(jnp.dot(w_all, v_ref[0, hh, pl.ds(k0s[u], n_blk * T), :],
                                     preferred_element_type=F32))
            out[u][1].append(carry)
        return out

    def largest(cs):
        m = cs[0]
        for c in cs[1:]:
            m = jnp.maximum(m, c)
        return jnp.max(m)

    def store(i, accs):
        q0 = _aligned(i * T, T)
        for n, hh in enumerate(heads):
            o_ref[0, pl.ds(q0, T), hh * SB_DIM:(hh + 1) * SB_DIM] = accs[n].astype(BF16)

    def finish(i, jb_first, accs, carries, cmax):
        q0 = _aligned(i * T, T)
        for n in range(len(accs)):
            acc_ref[n] = accs[n]
            carry_ref[n] = carries[n]

        def cond(st):
            jb, cmax = st
            return jnp.logical_and(jb >= 0, cmax > SB_UNDERFLOW)

        def body(st):
            jb, _ = st
            k0 = pl.multiple_of(jb * T, T)
            for n, hh in enumerate(heads):
                q = q_ref[0, hh, pl.ds(q0, T), :]
                z = lax.dot_general(q, k_ref[0, hh, pl.ds(k0, T), :], NT_DIMS,
                                    preferred_element_type=F32)
                sneg, later, total = _sb_tile(z, neg_tri2, None)
                w = jnp.exp((later + carry_ref[n]) - sneg).astype(BF16)
                acc_ref[n] += jnp.dot(w, v_ref[0, hh, pl.ds(k0, T), :],
                                      preferred_element_type=F32)
                carry_ref[n] += total
            return jb - 1, largest([carry_ref[n] for n in range(len(accs))])

        lax.while_loop(cond, body, (jb_first, cmax))
        store(i, [acc_ref[n] for n in range(len(accs))])

    def run(blocks, n_blk):
        results = windows(blocks, n_blk)
        for i, (accs, _) in zip(blocks, results):
            store(i, accs)
        firsts = [i - n_blk for i in blocks]
        if all(isinstance(jb, int) and jb < 0 for jb in firsts):
            return
        cmaxs = [largest(carries) for _, carries in results]
        unfinished = [jnp.logical_and(jb >= 0, cmax > SB_UNDERFLOW) for jb, cmax in zip(firsts, cmaxs)]

        @pl.when(functools.reduce(jnp.logical_or, unfinished))
        def _():
            for i, jb, (accs, carries), cmax, more in zip(blocks, firsts, results, cmaxs, unfinished):
                @pl.when(more)
                def _(i=i, jb=jb, accs=accs, carries=carries, cmax=cmax):
                    finish(i, jnp.int32(jb), accs, carries, cmax)

    n_win = min(SB_WINDOW, nq)
    for i in range(n_win - 1):
        run([i], i + 1)
    n_lead = (n_win - 1) + (nq - (n_win - 1)) % SB_QB
    if n_lead > n_win - 1:
        run(list(range(n_win - 1, n_lead)), n_win)

    def q_body(p, _):
        run([n_lead + p * SB_QB + u for u in range(SB_QB)], n_win)
        return 0

    lax.fori_loop(0, (nq - n_lead) // SB_QB, q_body, 0)


def _sb_attention(q, k, v):
    b, _, s, _ = q.shape
    hp = LANES // SB_DIM
    spec = pl.BlockSpec((1, hp, s, SB_DIM), lambda bi, h: (bi, h, 0, 0))
    return pl.pallas_call(
        _sb_kernel,
        grid=(b, SB_HEADS // hp),
        in_specs=[spec, spec, spec],
        out_specs=pl.BlockSpec((1, s, LANES), lambda bi, h: (bi, 0, h)),
        out_shape=jax.ShapeDtypeStruct((b, s, SB_WIDTH), BF16),
        scratch_shapes=[pltpu.VMEM((hp, SB_TQ, SB_DIM), F32),
                        pltpu.VMEM((hp, SB_TQ, SB_TQ), F32)],
        compiler_params=pltpu.CompilerParams(
            dimension_semantics=("parallel", "parallel"), vmem_limit_bytes=VMEM_LIMIT),
        name="sb_attention",
    )(q, k, v)


def _rows_bcast(a, offset):
    n_sub = a.shape[0] // GLA_SUB
    return jnp.concatenate(
        [jnp.broadcast_to(a[i * GLA_SUB + offset:i * GLA_SUB + offset + 1, :], (GLA_SUB, a.shape[1]))
         for i in range(n_sub)], axis=0)


def _gla_kernel(q_ref, k_ref, v_ref, go_ref, gl_ref, wup_ref, bup_ref, gn_ref, o_ref,
                st_ref, bc_ref, safe_ref):
    s_len = q_ref.shape[1]
    C = GLA_CHUNK
    G = GLA_GROUP
    n_sub = C // GLA_SUB
    scale = GLA_DK ** -0.5

    r_cc = lax.broadcasted_iota(jnp.int32, (C, C), 0)
    c_cc = lax.broadcasted_iota(jnp.int32, (C, C), 1)
    causal = c_cc <= r_cc
    ltri = causal.astype(BF16)
    ltri3 = jnp.concatenate([ltri, ltri, ltri], axis=1)
    diag_keep = jnp.logical_and((r_cc // GLA_SUB) == (c_cc // GLA_SUB), causal)
    r_loc = lax.broadcasted_iota(jnp.int32, (C, GLA_DK), 0) % GLA_SUB
    row_pick = [(r_loc == s).astype(BF16) for s in range(GLA_SUB)]

    n_groups = s_len // (C * G)
    gate_span = min(GLA_GATE_SPAN, n_groups)

    def gates(pi, _):
        t0s = [pl.multiple_of((pi * gate_span * G + u) * C, C) for u in range(gate_span * G)]
        xs = [jnp.dot(gl_ref[0, pl.ds(t0, C), :], wup_ref[...], preferred_element_type=F32)
              + bup_ref[...] for t0 in t0s]
        splits = []
        for x in xs:
            log_a = _log_sigmoid(x) * (1.0 / GLA_GATE_TAU)
            hi = log_a.astype(BF16)
            r1 = log_a - hi.astype(F32)
            mid = r1.astype(BF16)
            lo = (r1 - mid.astype(F32)).astype(BF16)
            splits.append(jnp.concatenate([hi, mid, lo], axis=0))
        spreads = []
        for t0, sp in zip(t0s, splits):
            bcum = jnp.dot(ltri3, sp, preferred_element_type=F32)
            bc_ref[pl.ds(t0, C), :] = bcum
            spreads.append(bcum[0:1, :] - bcum[C - 1:C, :])
        for g in range(gate_span):
            worst = functools.reduce(jnp.maximum, spreads[g * G:(g + 1) * G])
            safe_ref[pi * gate_span + g] = (jnp.max(worst) < GLA_SAFE_SPREAD).astype(jnp.int32)
        return 0

    def scores_factored(qs, ks, bcums):
        out = []
        for q, k, bcum in zip(qs, ks, bcums):
            r0 = bcum[0:1, :]
            qn = (q * jnp.exp(bcum - r0)).astype(BF16)
            kn = (k * jnp.exp(r0 - bcum)).astype(BF16)
            sc = lax.dot_general(qn, kn, NT_DIMS, preferred_element_type=F32)
            out.append(jnp.where(causal, sc, 0.0).astype(BF16))
        return out

    def scores_bounded(qs, ks, t0s, bcums):
        offs = []
        for q, k, bcum in zip(qs, ks, bcums):
            qn = (q * jnp.exp(bcum - _rows_bcast(bcum, 0))).astype(BF16)
            off = [jnp.zeros((GLA_SUB, C), F32)]
            for i in range(1, n_sub):
                n_prev = i * GLA_SUB
                ref_row = bcum[n_prev:n_prev + 1, :]
                kn = jnp.concatenate(
                    [(k[:n_prev] * jnp.exp(ref_row - bcum[:n_prev])).astype(BF16),
                     jnp.zeros((C - n_prev, GLA_DK), BF16)], axis=0)
                off.append(lax.dot_general(qn[i * GLA_SUB:(i + 1) * GLA_SUB], kn, NT_DIMS,
                                           preferred_element_type=F32))
            offs.append(jnp.concatenate(off, axis=0))
        out = []
        for q, t0, bcum, off in zip(qs, t0s, bcums, offs):
            kb = k_ref[0, pl.ds(t0, C), :]
            prods, keys = [], []
            for s in range(GLA_SUB):
                decay = jnp.exp(jnp.minimum(bcum - _rows_bcast(bcum, s), 0.0))
                prods.append((q * decay).astype(BF16))
                keys.append(kb * row_pick[s])
            diag = lax.dot_general(jnp.concatenate(prods, axis=1), jnp.concatenate(keys, axis=1),
                                   NT_DIMS, preferred_element_type=F32)
            out.append(jnp.where(diag_keep, diag, off).astype(BF16))
        return out

    def group(gi, factored):
        t0s = [pl.multiple_of((gi * G + u) * C, C) for u in range(G)]
        qs = [q_ref[0, pl.ds(t0, C), :].astype(F32) * scale for t0 in t0s]
        ks = [k_ref[0, pl.ds(t0, C), :].astype(F32) for t0 in t0s]
        vs = [v_ref[0, pl.ds(t0, C), :] for t0 in t0s]
        bcums = [bc_ref[pl.ds(t0, C), :] for t0 in t0s]
        scores = scores_factored(qs, ks, bcums) if factored else scores_bounded(qs, ks, t0s, bcums)

        kvs, q_ins, decays = [], [], []
        for q, k, v, bcum in zip(qs, ks, vs, bcums):
            b_last = bcum[C - 1:C, :]
            k_dec = (k * jnp.exp(b_last - bcum)).astype(BF16)
            kvs.append(lax.dot_general(v, k_dec, TN_DIMS, preferred_element_type=F32))
            q_ins.append((q * jnp.exp(bcum)).astype(BF16))
            decays.append(jnp.exp(b_last))
        st = st_ref[...]
        outs = []
        for q_in, kv, decay in zip(q_ins, kvs, decays):
            outs.append(lax.dot_general(q_in, st.astype(BF16), NT_DIMS, preferred_element_type=F32))
            st = st * decay + kv
        st_ref[...] = st

        for t0, o, sc, v in zip(t0s, outs, scores, vs):
            o = o + jnp.dot(sc, v, preferred_element_type=F32)
            y = _rms(o, gn_ref[...])
            g = go_ref[0, pl.ds(t0, C), :].astype(F32)
            o_ref[0, pl.ds(t0, C), :] = (y * (g * jax.nn.sigmoid(g))).astype(BF16)

    def main(gi, _):
        lax.cond(safe_ref[gi] == 1, lambda: group(gi, True), lambda: group(gi, False))
        return 0

    lax.fori_loop(0, n_groups // gate_span, gates, 0)
    st_ref[...] = jnp.zeros_like(st_ref)
    lax.fori_loop(0, n_groups, main, 0)


def _gla(gqk, gv, gout, glow, wup, bup, gnorm):
    b, s, _ = gv.shape
    n_groups, rem = divmod(s, GLA_CHUNK * GLA_GROUP)
    assert rem == 0 and n_groups % min(GLA_GATE_SPAN, n_groups) == 0, s
    return pl.pallas_call(
        _gla_kernel,
        grid=(b, GLA_HEADS),
        in_specs=[
            pl.BlockSpec((1, s, GLA_DK), lambda bi, h: (bi, 0, h)),
            pl.BlockSpec((1, s, GLA_DK), lambda bi, h: (bi, 0, GLA_HEADS + h)),
            pl.BlockSpec((1, s, GLA_DV), lambda bi, h: (bi, 0, h)),
            pl.BlockSpec((1, s, GLA_DV), lambda bi, h: (bi, 0, h)),
            pl.BlockSpec((1, s, LANES), lambda bi, h: (bi, 0, 0)),
            pl.BlockSpec((LANES, GLA_DK), lambda bi, h: (0, h)),
            pl.BlockSpec((1, GLA_DK), lambda bi, h: (0, h)),
            pl.BlockSpec((1, GLA_DV), lambda bi, h: (0, 0)),
        ],
        out_specs=pl.BlockSpec((1, s, GLA_DV), lambda bi, h: (bi, 0, h)),
        out_shape=jax.ShapeDtypeStruct((b, s, GLA_VALUE_WIDTH), BF16),
        scratch_shapes=[pltpu.VMEM((GLA_DV, GLA_DK), F32),
                        pltpu.VMEM((s, GLA_DK), F32),
                        pltpu.SMEM((s // (GLA_CHUNK * GLA_GROUP),), jnp.int32)],
        compiler_params=pltpu.CompilerParams(
            dimension_semantics=("parallel", "parallel"), vmem_limit_bytes=VMEM_LIMIT),
        name="gla",
    )(gqk, gqk, gv, gout, glow, wup, bup, gnorm)


def _tail_kernel(x_ref, osb_ref, ogla_ref, gates_ref, wpa_ref, wpb_ref, wout_ref,
                 g_ref, w1_ref, w2_ref, gf_ref, o_ref, *, ff_chunk):
    d = x_ref.shape[-1]
    a = jnp.dot(osb_ref[0], wpa_ref[...], preferred_element_type=F32)
    c = jnp.dot(ogla_ref[0], wpb_ref[...], preferred_element_type=F32)
    ga = jax.nn.sigmoid(gates_ref[0, :, 0:d].astype(F32))
    gb = jax.nn.sigmoid(gates_ref[0, :, d:2 * d].astype(F32))
    mix = ga * a + gb * c
    x = x_ref[0] + jnp.dot(mix.astype(BF16), wout_ref[...], preferred_element_type=F32)

    h = _rms(x, g_ref[...]).astype(BF16)
    acc = jnp.zeros_like(x)
    for j in range(w1_ref.shape[1] // ff_chunk):
        u = jnp.dot(h, w1_ref[:, j * ff_chunk:(j + 1) * ff_chunk], preferred_element_type=F32)
        u = jnp.square(jnp.maximum(u, 0.0)).astype(BF16)
        acc = acc + jnp.dot(u, w2_ref[j * ff_chunk:(j + 1) * ff_chunk, :], preferred_element_type=F32)
    o_ref[0] = _rms(x + acc, gf_ref[...])


def _tail(x, osb, ogla, gates, wpa, wpb, wout, g, w1, w2, gf, tm, ff_chunk):
    b, s, d = x.shape
    tok = lambda width: pl.BlockSpec((1, tm, width), lambda bi, i: (bi, i, 0))
    vec = pl.BlockSpec((1, d), lambda bi, i: (0, 0))
    full = lambda w: pl.BlockSpec(w.shape, lambda bi, i: (0, 0), pipeline_mode=pl.Buffered(1))
    return pl.pallas_call(
        functools.partial(_tail_kernel, ff_chunk=ff_chunk),
        grid=(b, s // tm),
        in_specs=[tok(d), tok(SB_WIDTH), tok(GLA_VALUE_WIDTH), tok(2 * d), full(wpa), full(wpb),
                  full(wout), vec, full(w1), full(w2), vec],
        out_specs=tok(d),
        out_shape=jax.ShapeDtypeStruct((b, s, d), F32),
        compiler_params=pltpu.CompilerParams(
            dimension_semantics=("parallel", "parallel"), vmem_limit_bytes=VMEM_LIMIT),
        name="merge_mlp",
    )(x, osb, ogla, gates, wpa, wpb, wout, g, w1, w2, gf)


def _layer(x, norm_mix, w_in, w_gate_up, b_gate_up, gla_norm, w_proj_sb, w_proj_gla, w_out,
           norm_mlp, w_ff1, w_ff2, final_gain, tm):
    d = x.shape[-1]
    c_low = 3 * SB_WIDTH + 2 * GLA_KEY_WIDTH + 2 * GLA_VALUE_WIDTH
    w_main = w_in[:, :c_low].astype(BF16)
    w_gates = w_in[:, c_low + GLA_GATE_RANK:].astype(BF16)
    w_low = jnp.pad(w_in[:, c_low:c_low + GLA_GATE_RANK].astype(BF16),
                    ((0, 0), (0, LANES - GLA_GATE_RANK)))
    wup = jnp.concatenate(
        [w_gate_up, jnp.zeros((LANES - GLA_GATE_RANK, GLA_KEY_WIDTH), w_gate_up.dtype)], axis=0).astype(BF16)

    sbq, sbk, sbv, gqk, gv, gout, gates, glow = _inproj(x, norm_mix.reshape(1, d), w_main, w_gates, w_low, tm)
    o_sb = _sb_attention(sbq, sbk, sbv)
    o_gla = _gla(gqk, gv, gout, glow, wup, b_gate_up.reshape(1, -1), gla_norm.reshape(1, -1))
    return _tail(x, o_sb, o_gla, gates, w_proj_sb.astype(BF16), w_proj_gla.astype(BF16),
                 w_out.astype(BF16), norm_mlp.reshape(1, d), w_ff1.astype(BF16), w_ff2.astype(BF16),
                 final_gain, tm, ff_chunk=1024)


def kernel(x, norm_mix, w_in, w_gate_up, b_gate_up, gla_norm, w_proj_sb, w_proj_gla, w_out,
           norm_mlp, w_ff1, w_ff2, norm_final):
    depth = w_in.shape[0]
    s = x.shape[1]
    tm = min(512, s)
    assert depth == 1, "the final RMSNorm is fused into the last layer's MLP kernel"
    return _layer(x, norm_mix[0], w_in[0], w_gate_up[0], b_gate_up[0], gla_norm[0], w_proj_sb[0],
                  w_proj_gla[0], w_out[0], norm_mlp[0], w_ff1[0], w_ff2[0],
                  norm_final.reshape(1, -1), tm)
```

```python
import functools

import jax
import jax.numpy as jnp
from jax import lax
from jax.experimental import pallas as pl
from jax.experimental.pallas import tpu as pltpu

F32 = jnp.float32
BF16 = jnp.bfloat16
EPS = 1e-6

SB_HEADS = 8
SB_DIM = 64
SB_WIDTH = SB_HEADS * SB_DIM
GLA_HEADS = 4
GLA_DK = 128
GLA_DV = 256
GLA_KEY_WIDTH = GLA_HEADS * GLA_DK
GLA_VALUE_WIDTH = GLA_HEADS * GLA_DV
GLA_GATE_RANK = 16
GLA_GATE_TAU = 16.0
GLA_CHUNK = 64
GLA_SUB = 16
GLA_GROUP = 8
GLA_GATE_SPAN = 4
GLA_SAFE_SPREAD = 60.0
LANES = 128

SB_TQ = 128
SB_WINDOW = 3
SB_QB = 4
SB_UNDERFLOW = -110.0

VMEM_LIMIT = 56 * 1024 * 1024

NT_DIMS = (((1,), (1,)), ((), ()))
TN_DIMS = (((0,), (0,)), ((), ()))


def _rms(x, g):
    ms = jnp.mean(x * x, axis=-1, keepdims=True)
    return (x * lax.rsqrt(ms + EPS)) * g


def _log_sigmoid(x):
    return -(jnp.maximum(-x, 0.0) + jnp.log(1.0 + jnp.exp(-jnp.abs(x))))


def _inproj_kernel(x_ref, g_ref, wa_ref, wg_ref, wl_ref, sbq_ref, sbk_ref, sbv_ref,
                   gqk_ref, gv_ref, gout_ref, gates_ref, glow_ref):
    h = _rms(x_ref[0], g_ref[...]).astype(BF16)

    def proj(w_ref, c0, width):
        return jnp.dot(h, w_ref[:, c0:c0 + width], preferred_element_type=F32)

    c = 0
    for idx, ref in enumerate((sbq_ref, sbk_ref, sbv_ref)):
        r = proj(wa_ref, c, SB_WIDTH)
        if idx == 0:
            r = r * (SB_DIM ** -0.5)
        ref[0] = r.astype(BF16)
        c += SB_WIDTH
    for ref, width in ((gqk_ref, 2 * GLA_KEY_WIDTH), (gv_ref, GLA_VALUE_WIDTH),
                       (gout_ref, GLA_VALUE_WIDTH)):
        ref[0] = proj(wa_ref, c, width).astype(BF16)
        c += width
    d_model = x_ref.shape[-1]
    for half in range(2):
        gates_ref[0, :, half * d_model:(half + 1) * d_model] = (
            proj(wg_ref, half * d_model, d_model).astype(BF16))
    glow_ref[0] = proj(wl_ref, 0, LANES).astype(BF16)


def _inproj(x, norm_g, w_main, w_gates, w_low, tm):
    b, s, d = x.shape
    grid = (b, s // tm)
    tok = lambda width: pl.BlockSpec((1, tm, width), lambda bi, i: (bi, i, 0))
    out_shape = (
        jax.ShapeDtypeStruct((b, s, SB_WIDTH), BF16),
        jax.ShapeDtypeStruct((b, s, SB_WIDTH), BF16),
        jax.ShapeDtypeStruct((b, s, SB_WIDTH), BF16),
        jax.ShapeDtypeStruct((b, s, 2 * GLA_KEY_WIDTH), BF16),
        jax.ShapeDtypeStruct((b, s, GLA_VALUE_WIDTH), BF16),
        jax.ShapeDtypeStruct((b, s, GLA_VALUE_WIDTH), BF16),
        jax.ShapeDtypeStruct((b, s, 2 * d), BF16),
        jax.ShapeDtypeStruct((b, s, LANES), BF16),
    )
    return pl.pallas_call(
        _inproj_kernel,
        grid=grid,
        in_specs=[
            tok(d),
            pl.BlockSpec((1, d), lambda bi, i: (0, 0)),
        ] + [pl.BlockSpec(w.shape, lambda bi, i: (0, 0), pipeline_mode=pl.Buffered(1))
             for w in (w_main, w_gates, w_low)],
        out_specs=(tok(SB_WIDTH), tok(SB_WIDTH), tok(SB_WIDTH), tok(2 * GLA_KEY_WIDTH), tok(GLA_VALUE_WIDTH),
                   tok(GLA_VALUE_WIDTH), tok(2 * d), tok(LANES)),
        out_shape=out_shape,
        compiler_params=pltpu.CompilerParams(
            dimension_semantics=("parallel", "parallel"), vmem_limit_bytes=VMEM_LIMIT),
        name="inproj",
    )(x, norm_g, w_main, w_gates, w_low)


def _aligned(start, tile):
    return start if isinstance(start, int) else pl.multiple_of(start, tile)


def _sb_tile(z, neg_tri2, mask):
    tk = z.shape[1]
    f = jnp.maximum(z, 0.0) + jnp.log(1.0 + jnp.exp(-jnp.abs(z)))
    if mask is not None:
        f = jnp.where(mask, f, 0.0)
    hi = f.astype(BF16)
    lo = (f - hi.astype(F32)).astype(BF16)
    cs = jnp.dot(jnp.concatenate([hi, lo], axis=1), neg_tri2, preferred_element_type=F32)
    return z + cs[:, :tk], cs[:, tk:]


def _sb_kernel(q_ref, k_ref, v_ref, o_ref, acc_ref, carry_ref):
    s_len = q_ref.shape[1]
    heads = range(q_ref.shape[2] // SB_DIM)
    T = SB_TQ
    nq = s_len // T
    lane_head = lax.broadcasted_iota(jnp.int32, (T, q_ref.shape[2]), 1) // SB_DIM
    row = lax.broadcasted_iota(jnp.int32, (T, T), 0)
    col = lax.broadcasted_iota(jnp.int32, (T, T), 1)
    neg_tri = jnp.where(row >= col, -1.0, 0.0).astype(BF16)
    half = jnp.concatenate([neg_tri, jnp.full((T, T), -1.0, BF16)], axis=1)
    neg_tri2 = jnp.concatenate([half, half], axis=0)
    past = col < row

    def windows(blocks, n_blk):
        chains = [(u, hh) for u in range(len(blocks)) for hh in heads]
        q0s = [_aligned(i * T, T) for i in blocks]
        k0s = [_aligned((i - (n_blk - 1)) * T, T) for i in blocks]
        zs = []
        for u, hh in chains:
            q = head_only(q_ref[0, pl.ds(q0s[u], T), :], hh)
            zs.append(lax.dot_general(q, k_ref[0, pl.ds(k0s[u], n_blk * T), :], NT_DIMS,
                                      preferred_element_type=F32))
        stats = []
        for z in zs:
            per_tile = []
            for jb in reversed(range(n_blk)):
                mask = past if jb == n_blk - 1 else None
                per_tile.append(_sb_tile(z[:, jb * T:(jb + 1) * T], neg_tri2, mask))
            stats.append(per_tile)
        out = [([], []) for _ in blocks]
        for (u, hh), per_tile in zip(chains, stats):
            carry = jnp.zeros((T, T), F32)
            ws = []
            for n, (logw, total) in enumerate(per_tile):
                w = jnp.exp(logw + carry)
                if n == 0:
                    w = jnp.where(past, w, 0.0)
                ws.append(w.astype(BF16))
                carry = carry + total
            w_all = ws[0] if n_blk == 1 else jnp.concatenate(ws[::-1], axis=1)
            out[u][0].append(jnp.dot(w_all, v_ref[0, pl.ds(k0s[u], n_blk * T), :],
                                     preferred_element_type=F32))
            out[u][1].append(carry)
        return out

    def largest(cs):
        m = cs[0]
        for c in cs[1:]:
            m = jnp.maximum(m, c)
        return jnp.max(m)

    def head_only(a, hh):
        return jnp.where(lane_head == hh, a, jnp.zeros_like(a))

    def store(i, accs):
        out = accs[0]
        for hh in list(heads)[1:]:
            out = jnp.where(lane_head == hh, accs[hh], out)
        o_ref[0, pl.ds(_aligned(i * T, T), T), :] = out.astype(BF16)

    def finish(i, jb_first, accs, carries, cmax):
        q0 = _aligned(i * T, T)
        for n in range(len(accs)):
            acc_ref[n] = accs[n]
            carry_ref[n] = carries[n]

        def cond(st):
            jb, cmax = st
            return jnp.logical_and(jb >= 0, cmax > SB_UNDERFLOW)

        def body(st):
            jb, _ = st
            k0 = pl.multiple_of(jb * T, T)
            for n, hh in enumerate(heads):
                q = head_only(q_ref[0, pl.ds(q0, T), :], hh)
                z = lax.dot_general(q, k_ref[0, pl.ds(k0, T), :], NT_DIMS,
                                    preferred_element_type=F32)
                logw, total = _sb_tile(z, neg_tri2, None)
                w = jnp.exp(logw + carry_ref[n]).astype(BF16)
                acc_ref[n] += jnp.dot(w, v_ref[0, pl.ds(k0, T), :],
                                      preferred_element_type=F32)
                carry_ref[n] += total
            return jb - 1, largest([carry_ref[n] for n in range(len(accs))])

        lax.while_loop(cond, body, (jb_first, cmax))
        store(i, [acc_ref[n] for n in range(len(accs))])

    def run(blocks, n_blk):
        results = windows(blocks, n_blk)
        for i, (accs, _) in zip(blocks, results):
            store(i, accs)
        firsts = [i - n_blk for i in blocks]
        if all(isinstance(jb, int) and jb < 0 for jb in firsts):
            return
        cmaxs = [largest(carries) for _, carries in results]
        unfinished = [jnp.logical_and(jb >= 0, cmax > SB_UNDERFLOW) for jb, cmax in zip(firsts, cmaxs)]

        @pl.when(functools.reduce(jnp.logical_or, unfinished))
        def _():
            for i, jb, (accs, carries), cmax, more in zip(blocks, firsts, results, cmaxs, unfinished):
                @pl.when(more)
                def _(i=i, jb=jb, accs=accs, carries=carries, cmax=cmax):
                    finish(i, jnp.int32(jb), accs, carries, cmax)

    n_win = min(SB_WINDOW, nq)
    for i in range(n_win - 1):
        run([i], i + 1)
    n_lead = (n_win - 1) + (nq - (n_win - 1)) % SB_QB
    if n_lead > n_win - 1:
        run(list(range(n_win - 1, n_lead)), n_win)

    def q_body(p, _):
        run([n_lead + p * SB_QB + u for u in range(SB_QB)], n_win)
        return 0

    lax.fori_loop(0, (nq - n_lead) // SB_QB, q_body, 0)


def _sb_attention(q, k, v):
    b, s, _ = q.shape
    hp = LANES // SB_DIM
    spec = pl.BlockSpec((1, s, LANES), lambda bi, h: (bi, 0, h))
    return pl.pallas_call(
        _sb_kernel,
        grid=(b, SB_HEADS // hp),
        in_specs=[spec, spec, spec],
        out_specs=spec,
        out_shape=jax.ShapeDtypeStruct((b, s, SB_WIDTH), BF16),
        scratch_shapes=[pltpu.VMEM((hp, SB_TQ, LANES), F32),
                        pltpu.VMEM((hp, SB_TQ, SB_TQ), F32)],
        compiler_params=pltpu.CompilerParams(
            dimension_semantics=("parallel", "parallel"), vmem_limit_bytes=VMEM_LIMIT),
        name="sb_attention",
    )(q, k, v)


def _rows_bcast(a, offset):
    n_sub = a.shape[0] // GLA_SUB
    return jnp.concatenate(
        [jnp.broadcast_to(a[i * GLA_SUB + offset:i * GLA_SUB + offset + 1, :], (GLA_SUB, a.shape[1]))
         for i in range(n_sub)], axis=0)


def _gla_kernel(q_ref, k_ref, v_ref, go_ref, gl_ref, wup_ref, bup_ref, gn_ref, o_ref,
                st_ref, bc_ref, safe_ref):
    s_len = q_ref.shape[1]
    C = GLA_CHUNK
    G = GLA_GROUP
    n_sub = C // GLA_SUB
    scale = GLA_DK ** -0.5

    r_cc = lax.broadcasted_iota(jnp.int32, (C, C), 0)
    c_cc = lax.broadcasted_iota(jnp.int32, (C, C), 1)
    causal = c_cc <= r_cc
    ltri = causal.astype(BF16)
    ltri3 = jnp.concatenate([ltri, ltri, ltri], axis=1)
    diag_keep = jnp.logical_and((r_cc // GLA_SUB) == (c_cc // GLA_SUB), causal)
    r_loc = lax.broadcasted_iota(jnp.int32, (C, GLA_DK), 0) % GLA_SUB
    row_pick = [(r_loc == s).astype(BF16) for s in range(GLA_SUB)]

    n_groups = s_len // (C * G)
    gate_span = min(GLA_GATE_SPAN, n_groups)

    def gates(pi, _):
        t0s = [pl.multiple_of((pi * gate_span * G + u) * C, C) for u in range(gate_span * G)]
        xs = [jnp.dot(gl_ref[0, pl.ds(t0, C), :], wup_ref[...], preferred_element_type=F32)
              + bup_ref[...] for t0 in t0s]
        splits = []
        for x in xs:
            log_a = _log_sigmoid(x) * (1.0 / GLA_GATE_TAU)
            hi = log_a.astype(BF16)
            r1 = log_a - hi.astype(F32)
            mid = r1.astype(BF16)
            lo = (r1 - mid.astype(F32)).astype(BF16)
            splits.append(jnp.concatenate([hi, mid, lo], axis=0))
        spreads = []
        for t0, sp in zip(t0s, splits):
            bcum = jnp.dot(ltri3, sp, preferred_element_type=F32)
            bc_ref[pl.ds(t0, C), :] = bcum
            spreads.append(bcum[0:1, :] - bcum[C - 1:C, :])
        for g in range(gate_span):
            worst = functools.reduce(jnp.maximum, spreads[g * G:(g + 1) * G])
            safe_ref[pi * gate_span + g] = (jnp.max(worst) < GLA_SAFE_SPREAD).astype(jnp.int32)
        return 0

    def scores_factored(qs, ks, bcums):
        out = []
        for q, k, bcum in zip(qs, ks, bcums):
            r0 = bcum[0:1, :]
            qn = (q * jnp.exp(bcum - r0)).astype(BF16)
            kn = (k * jnp.exp(r0 - bcum)).astype(BF16)
            sc = lax.dot_general(qn, kn, NT_DIMS, preferred_element_type=F32)
            out.append(jnp.where(causal, sc, 0.0).astype(BF16))
        return out

    def scores_bounded(qs, ks, t0s, bcums):
        offs = []
        for q, k, bcum in zip(qs, ks, bcums):
            qn = (q * jnp.exp(bcum - _rows_bcast(bcum, 0))).astype(BF16)
            off = [jnp.zeros((GLA_SUB, C), F32)]
            for i in range(1, n_sub):
                n_prev = i * GLA_SUB
                ref_row = bcum[n_prev:n_prev + 1, :]
                kn = jnp.concatenate(
                    [(k[:n_prev] * jnp.exp(ref_row - bcum[:n_prev])).astype(BF16),
                     jnp.zeros((C - n_prev, GLA_DK), BF16)], axis=0)
                off.append(lax.dot_general(qn[i * GLA_SUB:(i + 1) * GLA_SUB], kn, NT_DIMS,
                                           preferred_element_type=F32))
            offs.append(jnp.concatenate(off, axis=0))
        out = []
        for q, t0, bcum, off in zip(qs, t0s, bcums, offs):
            kb = k_ref[0, pl.ds(t0, C), :]
            prods, keys = [], []
            for s in range(GLA_SUB):
                decay = jnp.exp(jnp.minimum(bcum - _rows_bcast(bcum, s), 0.0))
                prods.append((q * decay).astype(BF16))
                keys.append(kb * row_pick[s])
            diag = lax.dot_general(jnp.concatenate(prods, axis=1), jnp.concatenate(keys, axis=1),
                                   NT_DIMS, preferred_element_type=F32)
            out.append(jnp.where(diag_keep, diag, off).astype(BF16))
        return out

    def group(gi, factored):
        t0s = [pl.multiple_of((gi * G + u) * C, C) for u in range(G)]
        qs = [q_ref[0, pl.ds(t0, C), :].astype(F32) * scale for t0 in t0s]
        ks = [k_ref[0, pl.ds(t0, C), :].astype(F32) for t0 in t0s]
        vs = [v_ref[0, pl.ds(t0, C), :] for t0 in t0s]
        bcums = [bc_ref[pl.ds(t0, C), :] for t0 in t0s]
        scores = scores_factored(qs, ks, bcums) if factored else scores_bounded(qs, ks, t0s, bcums)

        kvs, q_ins, decays = [], [], []
        for q, k, v, bcum in zip(qs, ks, vs, bcums):
            b_last = bcum[C - 1:C, :]
            k_dec = (k * jnp.exp(b_last - bcum)).astype(BF16)
            kvs.append(lax.dot_general(v, k_dec, TN_DIMS, preferred_element_type=F32))
            q_ins.append((q * jnp.exp(bcum)).astype(BF16))
            decays.append(jnp.exp(b_last))
        st = st_ref[...]
        outs = []
        for q_in, kv, decay in zip(q_ins, kvs, decays):
            outs.append(lax.dot_general(q_in, st.astype(BF16), NT_DIMS, preferred_element_type=F32))
            st = st * decay + kv
        st_ref[...] = st

        for t0, o, sc, v in zip(t0s, outs, scores, vs):
            o = o + jnp.dot(sc, v, preferred_element_type=F32)
            y = _rms(o, gn_ref[...])
            g = go_ref[0, pl.ds(t0, C), :].astype(F32)
            o_ref[0, pl.ds(t0, C), :] = (y * (g * jax.nn.sigmoid(g))).astype(BF16)

    def main(gi, _):
        lax.cond(safe_ref[gi] == 1, lambda: group(gi, True), lambda: group(gi, False))
        return 0

    lax.fori_loop(0, n_groups // gate_span, gates, 0)
    st_ref[...] = jnp.zeros_like(st_ref)
    lax.fori_loop(0, n_groups, main, 0)


def _gla(gqk, gv, gout, glow, wup, bup, gnorm):
    b, s, _ = gv.shape
    n_groups, rem = divmod(s, GLA_CHUNK * GLA_GROUP)
    assert rem == 0 and n_groups % min(GLA_GATE_SPAN, n_groups) == 0, s
    return pl.pallas_call(
        _gla_kernel,
        grid=(b, GLA_HEADS),
        in_specs=[
            pl.BlockSpec((1, s, GLA_DK), lambda bi, h: (bi, 0, h)),
            pl.BlockSpec((1, s, GLA_DK), lambda bi, h: (bi, 0, GLA_HEADS + h)),
            pl.BlockSpec((1, s, GLA_DV), lambda bi, h: (bi, 0, h)),
            pl.BlockSpec((1, s, GLA_DV), lambda bi, h: (bi, 0, h)),
            pl.BlockSpec((1, s, LANES), lambda bi, h: (bi, 0, 0)),
            pl.BlockSpec((LANES, GLA_DK), lambda bi, h: (0, h)),
            pl.BlockSpec((1, GLA_DK), lambda bi, h: (0, h)),
            pl.BlockSpec((1, GLA_DV), lambda bi, h: (0, 0)),
        ],
        out_specs=pl.BlockSpec((1, s, GLA_DV), lambda bi, h: (bi, 0, h)),
        out_shape=jax.ShapeDtypeStruct((b, s, GLA_VALUE_WIDTH), BF16),
        scratch_shapes=[pltpu.VMEM((GLA_DV, GLA_DK), F32),
                        pltpu.VMEM((s, GLA_DK), F32),
                        pltpu.SMEM((s // (GLA_CHUNK * GLA_GROUP),), jnp.int32)],
        compiler_params=pltpu.CompilerParams(
            dimension_semantics=("parallel", "parallel"), vmem_limit_bytes=VMEM_LIMIT),
        name="gla",
    )(gqk, gqk, gv, gout, glow, wup, bup, gnorm)


def _tail_kernel(x_ref, osb_ref, ogla_ref, gates_ref, wpa_ref, wpb_ref, wout_ref,
                 g_ref, w1_ref, w2_ref, gf_ref, o_ref, *, ff_chunk):
    d = x_ref.shape[-1]
    a = jnp.dot(osb_ref[0], wpa_ref[...], preferred_element_type=F32)
    c = jnp.dot(ogla_ref[0], wpb_ref[...], preferred_element_type=F32)
    ga = jax.nn.sigmoid(gates_ref[0, :, 0:d].astype(F32))
    gb = jax.nn.sigmoid(gates_ref[0, :, d:2 * d].astype(F32))
    mix = ga * a + gb * c
    x = x_ref[0] + jnp.dot(mix.astype(BF16), wout_ref[...], preferred_element_type=F32)

    h = _rms(x, g_ref[...]).astype(BF16)
    acc = jnp.zeros_like(x)
    for j in range(w1_ref.shape[1] // ff_chunk):
        u = jnp.dot(h, w1_ref[:, j * ff_chunk:(j + 1) * ff_chunk], preferred_element_type=F32)
        u = jnp.square(jnp.maximum(u, 0.0)).astype(BF16)
        acc = acc + jnp.dot(u, w2_ref[j * ff_chunk:(j + 1) * ff_chunk, :], preferred_element_type=F32)
    o_ref[0] = _rms(x + acc, gf_ref[...])


def _tail(x, osb, ogla, gates, wpa, wpb, wout, g, w1, w2, gf, tm, ff_chunk):
    b, s, d = x.shape
    tok = lambda width: pl.BlockSpec((1, tm, width), lambda bi, i: (bi, i, 0))
    vec = pl.BlockSpec((1, d), lambda bi, i: (0, 0))
    full = lambda w: pl.BlockSpec(w.shape, lambda bi, i: (0, 0), pipeline_mode=pl.Buffered(1))
    return pl.pallas_call(
        functools.partial(_tail_kernel, ff_chunk=ff_chunk),
        grid=(b, s // tm),
        in_specs=[tok(d), tok(SB_WIDTH), tok(GLA_VALUE_WIDTH), tok(2 * d), full(wpa), full(wpb),
                  full(wout), vec, full(w1), full(w2), vec],
        out_specs=tok(d),
        out_shape=jax.ShapeDtypeStruct((b, s, d), F32),
        compiler_params=pltpu.CompilerParams(
            dimension_semantics=("parallel", "parallel"), vmem_limit_bytes=VMEM_LIMIT),
        name="merge_mlp",
    )(x, osb, ogla, gates, wpa, wpb, wout, g, w1, w2, gf)


def _layer(x, norm_mix, w_in, w_gate_up, b_gate_up, gla_norm, w_proj_sb, w_proj_gla, w_out,
           norm_mlp, w_ff1, w_ff2, final_gain, tm):
    d = x.shape[-1]
    c_low = 3 * SB_WIDTH + 2 * GLA_KEY_WIDTH + 2 * GLA_VALUE_WIDTH
    w_main = w_in[:, :c_low].astype(BF16)
    w_gates = w_in[:, c_low + GLA_GATE_RANK:].astype(BF16)
    w_low = jnp.pad(w_in[:, c_low:c_low + GLA_GATE_RANK].astype(BF16),
                    ((0, 0), (0, LANES - GLA_GATE_RANK)))
    wup = jnp.concatenate(
        [w_gate_up, jnp.zeros((LANES - GLA_GATE_RANK, GLA_KEY_WIDTH), w_gate_up.dtype)], axis=0).astype(BF16)

    sbq, sbk, sbv, gqk, gv, gout, gates, glow = _inproj(x, norm_mix.reshape(1, d), w_main, w_gates, w_low, tm)
    o_sb = _sb_attention(sbq, sbk, sbv)
    o_gla = _gla(gqk, gv, gout, glow, wup, b_gate_up.reshape(1, -1), gla_norm.reshape(1, -1))
    return _tail(x, o_sb, o_gla, gates, w_proj_sb.astype(BF16), w_proj_gla.astype(BF16),
                 w_out.astype(BF16), norm_mlp.reshape(1, d), w_ff1.astype(BF16), w_ff2.astype(BF16),
                 final_gain, tm, ff_chunk=1024)


def kernel(x, norm_mix, w_in, w_gate_up, b_gate_up, gla_norm, w_proj_sb, w_proj_gla, w_out,
           norm_mlp, w_ff1, w_ff2, norm_final):
    depth = w_in.shape[0]
    s = x.shape[1]
    tm = min(512, s)
    assert depth == 1, "the final RMSNorm is fused into the last layer's MLP kernel"
    return _layer(x, norm_mix[0], w_in[0], w_gate_up[0], b_gate_up[0], gla_norm[0], w_proj_sb[0],
                  w_proj_gla[0], w_out[0], norm_mlp[0], w_ff1[0], w_ff2[0],
                  norm_final.reshape(1, -1), tm)
```

```python
import functools

import jax
import jax.numpy as jnp
from jax import lax
from jax.experimental import pallas as pl
from jax.experimental.pallas import tpu as pltpu

F32 = jnp.float32
BF16 = jnp.bfloat16
EPS = 1e-6

SB_HEADS = 8
SB_DIM = 64
SB_WIDTH = SB_HEADS * SB_DIM
GLA_HEADS = 4
GLA_DK = 128
GLA_DV = 256
GLA_KEY_WIDTH = GLA_HEADS * GLA_DK
GLA_VALUE_WIDTH = GLA_HEADS * GLA_DV
GLA_GATE_RANK = 16
GLA_GATE_TAU = 16.0
GLA_CHUNK = 64
GLA_SUB = 16
GLA_GROUP = 8
GLA_SAFE_SPREAD = 60.0
LANES = 128

SB_TQ = 128
SB_WINDOW = 3
SB_QB = 4
SB_UNDERFLOW = -110.0

VMEM_LIMIT = 56 * 1024 * 1024

NT_DIMS = (((1,), (1,)), ((), ()))
TN_DIMS = (((0,), (0,)), ((), ()))


def _rms(x, g):
    ms = jnp.mean(x * x, axis=-1, keepdims=True)
    return (x * lax.rsqrt(ms + EPS)) * g


def _log_sigmoid(x):
    return -(jnp.maximum(-x, 0.0) + jnp.log(1.0 + jnp.exp(-jnp.abs(x))))


def _inproj_kernel(x_ref, g_ref, wa_ref, wg_ref, wl_ref, wup_ref, bup_ref, sbq_ref, sbk_ref, sbv_ref,
                   gqk_ref, gv_ref, gout_ref, gates_ref, bc_ref):
    h = _rms(x_ref[0], g_ref[...]).astype(BF16)

    def proj(w_ref, c0, width):
        return jnp.dot(h, w_ref[:, c0:c0 + width], preferred_element_type=F32)

    def sb_proj(idx, ref):
        r = proj(wa_ref, idx * SB_WIDTH, SB_WIDTH)
        if idx == 0:
            r = r * (SB_DIM ** -0.5)
        ref[0] = r.astype(BF16)

    low = proj(wl_ref, 0, LANES).astype(BF16)
    sb_proj(0, sbq_ref)
    gate = jnp.dot(low, wup_ref[...], preferred_element_type=F32) + bup_ref[...]
    sb_proj(1, sbk_ref)
    log_a = _log_sigmoid(gate) * (1.0 / GLA_GATE_TAU)
    hi = log_a.astype(BF16)
    r1 = log_a - hi.astype(F32)
    mid = r1.astype(BF16)
    lo = (r1 - mid.astype(F32)).astype(BF16)
    sb_proj(2, sbv_ref)
    c = 3 * SB_WIDTH
    gqk_ref[0] = proj(wa_ref, c, 2 * GLA_KEY_WIDTH).astype(BF16)
    c += 2 * GLA_KEY_WIDTH
    C = GLA_CHUNK
    row = lax.broadcasted_iota(jnp.int32, (C, 3 * C), 0)
    col = lax.broadcasted_iota(jnp.int32, (C, 3 * C), 1)
    ltri3 = ((col % C) <= row).astype(BF16)
    for c0 in range(0, log_a.shape[0], C):
        parts = jnp.concatenate([p[c0:c0 + C] for p in (hi, mid, lo)], axis=0)
        bc_ref[0, c0:c0 + C, :] = jnp.dot(ltri3, parts, preferred_element_type=F32)
    for ref in (gv_ref, gout_ref):
        ref[0] = proj(wa_ref, c, GLA_VALUE_WIDTH).astype(BF16)
        c += GLA_VALUE_WIDTH
    d_model = x_ref.shape[-1]
    for half in range(2):
        gates_ref[0, :, half * d_model:(half + 1) * d_model] = (
            proj(wg_ref, half * d_model, d_model).astype(BF16))


def _inproj(x, norm_g, w_main, w_gates, w_low, wup, bup, tm):
    b, s, d = x.shape
    assert tm % GLA_CHUNK == 0
    grid = (b, s // tm)
    tok = lambda width: pl.BlockSpec((1, tm, width), lambda bi, i: (bi, i, 0))
    out_shape = (
        jax.ShapeDtypeStruct((b, s, SB_WIDTH), BF16),
        jax.ShapeDtypeStruct((b, s, SB_WIDTH), BF16),
        jax.ShapeDtypeStruct((b, s, SB_WIDTH), BF16),
        jax.ShapeDtypeStruct((b, s, 2 * GLA_KEY_WIDTH), BF16),
        jax.ShapeDtypeStruct((b, s, GLA_VALUE_WIDTH), BF16),
        jax.ShapeDtypeStruct((b, s, GLA_VALUE_WIDTH), BF16),
        jax.ShapeDtypeStruct((b, s, 2 * d), BF16),
        jax.ShapeDtypeStruct((b, s, GLA_KEY_WIDTH), F32),
    )
    return pl.pallas_call(
        _inproj_kernel,
        grid=grid,
        in_specs=[
            tok(d),
            pl.BlockSpec((1, d), lambda bi, i: (0, 0)),
        ] + [pl.BlockSpec(w.shape, lambda bi, i: (0, 0), pipeline_mode=pl.Buffered(1))
             for w in (w_main, w_gates, w_low, wup, bup)],
        out_specs=(tok(SB_WIDTH), tok(SB_WIDTH), tok(SB_WIDTH), tok(2 * GLA_KEY_WIDTH), tok(GLA_VALUE_WIDTH),
                   tok(GLA_VALUE_WIDTH), tok(2 * d), tok(GLA_KEY_WIDTH)),
        out_shape=out_shape,
        compiler_params=pltpu.CompilerParams(
            dimension_semantics=("parallel", "parallel"), vmem_limit_bytes=VMEM_LIMIT),
        name="inproj",
    )(x, norm_g, w_main, w_gates, w_low, wup, bup)


def _aligned(start, tile):
    return start if isinstance(start, int) else pl.multiple_of(start, tile)


def _sb_tile(z, neg_tri2, mask):
    tk = z.shape[1]
    f = jnp.maximum(z, 0.0) + jnp.log(1.0 + jnp.exp(-jnp.abs(z)))
    if mask is not None:
        f = jnp.where(mask, f, 0.0)
    hi = f.astype(BF16)
    lo = (f - hi.astype(F32)).astype(BF16)
    cs = jnp.dot(jnp.concatenate([hi, lo], axis=1), neg_tri2, preferred_element_type=F32)
    return z + cs[:, :tk], cs[:, tk:]


def _sb_kernel(q_ref, k_ref, v_ref, o_ref, acc_ref, carry_ref):
    s_len = q_ref.shape[1]
    heads = range(q_ref.shape[2] // SB_DIM)
    T = SB_TQ
    nq = s_len // T
    lane_head = lax.broadcasted_iota(jnp.int32, (T, q_ref.shape[2]), 1) // SB_DIM
    row = lax.broadcasted_iota(jnp.int32, (T, T), 0)
    col = lax.broadcasted_iota(jnp.int32, (T, T), 1)
    neg_tri = jnp.where(row >= col, -1.0, 0.0).astype(BF16)
    half = jnp.concatenate([neg_tri, jnp.full((T, T), -1.0, BF16)], axis=1)
    neg_tri2 = jnp.concatenate([half, half], axis=0)
    past = col < row

    def windows(blocks, n_blk):
        chains = [(u, hh) for u in range(len(blocks)) for hh in heads]
        q0s = [_aligned(i * T, T) for i in blocks]
        k0s = [_aligned((i - (n_blk - 1)) * T, T) for i in blocks]
        zs = []
        for u, hh in chains:
            q = head_only(q_ref[0, pl.ds(q0s[u], T), :], hh)
            zs.append(lax.dot_general(q, k_ref[0, pl.ds(k0s[u], n_blk * T), :], NT_DIMS,
                                      preferred_element_type=F32))
        stats = []
        for z in zs:
            per_tile = []
            for jb in reversed(range(n_blk)):
                mask = past if jb == n_blk - 1 else None
                per_tile.append(_sb_tile(z[:, jb * T:(jb + 1) * T], neg_tri2, mask))
            stats.append(per_tile)
        out = [([], []) for _ in blocks]
        for (u, hh), per_tile in zip(chains, stats):
            carry = jnp.zeros((T, T), F32)
            ws = []
            for n, (logw, total) in enumerate(per_tile):
                w = jnp.exp(logw + carry)
                if n == 0:
                    w = jnp.where(past, w, 0.0)
                ws.append(w.astype(BF16))
                carry = carry + total
            w_all = ws[0] if n_blk == 1 else jnp.concatenate(ws[::-1], axis=1)
            out[u][0].append(jnp.dot(w_all, v_ref[0, pl.ds(k0s[u], n_blk * T), :],
                                     preferred_element_type=F32))
            out[u][1].append(carry)
        return out

    def largest(cs):
        m = cs[0]
        for c in cs[1:]:
            m = jnp.maximum(m, c)
        return jnp.max(m)

    def head_only(a, hh):
        return jnp.where(lane_head == hh, a, jnp.zeros_like(a))

    def store(i, accs):
        out = accs[0]
        for hh in list(heads)[1:]:
            out = jnp.where(lane_head == hh, accs[hh], out)
        o_ref[0, pl.ds(_aligned(i * T, T), T), :] = out.astype(BF16)

    def finish(i, jb_first, accs, carries, cmax):
        q0 = _aligned(i * T, T)
        for n in range(len(accs)):
            acc_ref[n] = accs[n]
            carry_ref[n] = carries[n]

        def cond(st):
            jb, cmax = st
            return jnp.logical_and(jb >= 0, cmax > SB_UNDERFLOW)

        def body(st):
            jb, _ = st
            k0 = pl.multiple_of(jb * T, T)
            for n, hh in enumerate(heads):
                q = head_only(q_ref[0, pl.ds(q0, T), :], hh)
                z = lax.dot_general(q, k_ref[0, pl.ds(k0, T), :], NT_DIMS,
                                    preferred_element_type=F32)
                logw, total = _sb_tile(z, neg_tri2, None)
                w = jnp.exp(logw + carry_ref[n]).astype(BF16)
                acc_ref[n] += jnp.dot(w, v_ref[0, pl.ds(k0, T), :],
                                      preferred_element_type=F32)
                carry_ref[n] += total
            return jb - 1, largest([carry_ref[n] for n in range(len(accs))])

        lax.while_loop(cond, body, (jb_first, cmax))
        store(i, [acc_ref[n] for n in range(len(accs))])

    def run(blocks, n_blk):
        results = windows(blocks, n_blk)
        for i, (accs, _) in zip(blocks, results):
            store(i, accs)
        firsts = [i - n_blk for i in blocks]
        if all(isinstance(jb, int) and jb < 0 for jb in firsts):
            return
        cmaxs = [largest(carries) for _, carries in results]
        unfinished = [jnp.logical_and(jb >= 0, cmax > SB_UNDERFLOW) for jb, cmax in zip(firsts, cmaxs)]

        @pl.when(functools.reduce(jnp.logical_or, unfinished))
        def _():
            for i, jb, (accs, carries), cmax, more in zip(blocks, firsts, results, cmaxs, unfinished):
                @pl.when(more)
                def _(i=i, jb=jb, accs=accs, carries=carries, cmax=cmax):
                    finish(i, jnp.int32(jb), accs, carries, cmax)

    n_win = min(SB_WINDOW, nq)
    for i in range(n_win - 1):
        run([i], i + 1)
    n_lead = (n_win - 1) + (nq - (n_win - 1)) % SB_QB
    if n_lead > n_win - 1:
        run(list(range(n_win - 1, n_lead)), n_win)

    def q_body(p, _):
        run([n_lead + p * SB_QB + u for u in range(SB_QB)], n_win)
        return 0

    lax.fori_loop(0, (nq - n_lead) // SB_QB, q_body, 0)


def _sb_attention(q, k, v):
    b, s, _ = q.shape
    hp = LANES // SB_DIM
    spec = pl.BlockSpec((1, s, LANES), lambda bi, h: (bi, 0, h))
    return pl.pallas_call(
        _sb_kernel,
        grid=(b, SB_HEADS // hp),
        in_specs=[spec, spec, spec],
        out_specs=spec,
        out_shape=jax.ShapeDtypeStruct((b, s, SB_WIDTH), BF16),
        scratch_shapes=[pltpu.VMEM((hp, SB_TQ, LANES), F32),
                        pltpu.VMEM((hp, SB_TQ, SB_TQ), F32)],
        compiler_params=pltpu.CompilerParams(
            dimension_semantics=("parallel", "parallel"), vmem_limit_bytes=VMEM_LIMIT),
        name="sb_attention",
    )(q, k, v)


def _rows_bcast(a, offset):
    n_sub = a.shape[0] // GLA_SUB
    return jnp.concatenate(
        [jnp.broadcast_to(a[i * GLA_SUB + offset:i * GLA_SUB + offset + 1, :], (GLA_SUB, a.shape[1]))
         for i in range(n_sub)], axis=0)


def _gla_kernel(q_ref, k_ref, v_ref, go_ref, bc_ref, gn_ref, o_ref, st_ref, safe_ref):
    s_len = q_ref.shape[1]
    C = GLA_CHUNK
    G = GLA_GROUP
    n_sub = C // GLA_SUB
    scale = GLA_DK ** -0.5

    r_cc = lax.broadcasted_iota(jnp.int32, (C, C), 0)
    c_cc = lax.broadcasted_iota(jnp.int32, (C, C), 1)
    causal = c_cc <= r_cc
    diag_keep = jnp.logical_and((r_cc // GLA_SUB) == (c_cc // GLA_SUB), causal)
    r_loc = lax.broadcasted_iota(jnp.int32, (C, GLA_DK), 0) % GLA_SUB
    row_pick = [(r_loc == s).astype(BF16) for s in range(GLA_SUB)]

    n_groups = s_len // (C * G)
    n_chunks = s_len // C
    spread = (bc_ref[0, pl.ds(0, n_chunks, stride=C), :]
              - bc_ref[0, pl.ds(C - 1, n_chunks, stride=C), :])
    for g in range(n_groups):
        safe_ref[g] = (jnp.max(spread[g * G:(g + 1) * G]) < GLA_SAFE_SPREAD).astype(jnp.int32)

    def scores_factored(qs, ks, bcums):
        out = []
        for q, k, bcum in zip(qs, ks, bcums):
            r0 = bcum[0:1, :]
            qn = (q * jnp.exp(bcum - r0)).astype(BF16)
            kn = (k * jnp.exp(r0 - bcum)).astype(BF16)
            sc = lax.dot_general(qn, kn, NT_DIMS, preferred_element_type=F32)
            out.append(jnp.where(causal, sc, 0.0).astype(BF16))
        return out

    def scores_bounded(qs, ks, t0s, bcums):
        offs = []
        for q, k, bcum in zip(qs, ks, bcums):
            qn = (q * jnp.exp(bcum - _rows_bcast(bcum, 0))).astype(BF16)
            off = [jnp.zeros((GLA_SUB, C), F32)]
            for i in range(1, n_sub):
                n_prev = i * GLA_SUB
                ref_row = bcum[n_prev:n_prev + 1, :]
                kn = jnp.concatenate(
                    [(k[:n_prev] * jnp.exp(ref_row - bcum[:n_prev])).astype(BF16),
                     jnp.zeros((C - n_prev, GLA_DK), BF16)], axis=0)
                off.append(lax.dot_general(qn[i * GLA_SUB:(i + 1) * GLA_SUB], kn, NT_DIMS,
                                           preferred_element_type=F32))
            offs.append(jnp.concatenate(off, axis=0))
        out = []
        for q, t0, bcum, off in zip(qs, t0s, bcums, offs):
            kb = k_ref[0, pl.ds(t0, C), :]
            prods, keys = [], []
            for s in range(GLA_SUB):
                decay = jnp.exp(jnp.minimum(bcum - _rows_bcast(bcum, s), 0.0))
                prods.append((q * decay).astype(BF16))
                keys.append(kb * row_pick[s])
            diag = lax.dot_general(jnp.concatenate(prods, axis=1), jnp.concatenate(keys, axis=1),
                                   NT_DIMS, preferred_element_type=F32)
            out.append(jnp.where(diag_keep, diag, off).astype(BF16))
        return out

    def group(gi, factored):
        t0s = [pl.multiple_of((gi * G + u) * C, C) for u in range(G)]
        qs = [q_ref[0, pl.ds(t0, C), :].astype(F32) * scale for t0 in t0s]
        ks = [k_ref[0, pl.ds(t0, C), :].astype(F32) for t0 in t0s]
        vs = [v_ref[0, pl.ds(t0, C), :] for t0 in t0s]
        bcums = [bc_ref[0, pl.ds(t0, C), :] for t0 in t0s]
        scores = scores_factored(qs, ks, bcums) if factored else scores_bounded(qs, ks, t0s, bcums)

        kvs, q_ins, decays = [], [], []
        for q, k, v, bcum in zip(qs, ks, vs, bcums):
            b_last = bcum[C - 1:C, :]
            k_dec = (k * jnp.exp(b_last - bcum)).astype(BF16)
            kvs.append(lax.dot_general(v, k_dec, TN_DIMS, preferred_element_type=F32))
            q_ins.append((q * jnp.exp(bcum)).astype(BF16))
            decays.append(jnp.exp(b_last))
        st = st_ref[...]
        outs = []
        for q_in, kv, decay in zip(q_ins, kvs, decays):
            outs.append(lax.dot_general(q_in, st.astype(BF16), NT_DIMS, preferred_element_type=F32))
            st = st * decay + kv
        st_ref[...] = st

        for t0, o, sc, v in zip(t0s, outs, scores, vs):
            o = o + jnp.dot(sc, v, preferred_element_type=F32)
            y = _rms(o, gn_ref[...])
            g = go_ref[0, pl.ds(t0, C), :].astype(F32)
            o_ref[0, pl.ds(t0, C), :] = (y * (g * jax.nn.sigmoid(g))).astype(BF16)

    def main(gi, _):
        lax.cond(safe_ref[gi] == 1, lambda: group(gi, True), lambda: group(gi, False))
        return 0

    st_ref[...] = jnp.zeros_like(st_ref)
    lax.fori_loop(0, n_groups, main, 0)


def _gla(gqk, gv, gout, bcum, gnorm):
    b, s, _ = gv.shape
    n_groups, rem = divmod(s, GLA_CHUNK * GLA_GROUP)
    assert rem == 0, s
    return pl.pallas_call(
        _gla_kernel,
        grid=(b, GLA_HEADS),
        in_specs=[
            pl.BlockSpec((1, s, GLA_DK), lambda bi, h: (bi, 0, h)),
            pl.BlockSpec((1, s, GLA_DK), lambda bi, h: (bi, 0, GLA_HEADS + h)),
            pl.BlockSpec((1, s, GLA_DV), lambda bi, h: (bi, 0, h)),
            pl.BlockSpec((1, s, GLA_DV), lambda bi, h: (bi, 0, h)),
            pl.BlockSpec((1, s, GLA_DK), lambda bi, h: (bi, 0, h)),
            pl.BlockSpec((1, GLA_DV), lambda bi, h: (0, 0)),
        ],
        out_specs=pl.BlockSpec((1, s, GLA_DV), lambda bi, h: (bi, 0, h)),
        out_shape=jax.ShapeDtypeStruct((b, s, GLA_VALUE_WIDTH), BF16),
        scratch_shapes=[pltpu.VMEM((GLA_DV, GLA_DK), F32),
                        pltpu.SMEM((n_groups,), jnp.int32)],
        compiler_params=pltpu.CompilerParams(
            dimension_semantics=("parallel", "parallel"), vmem_limit_bytes=VMEM_LIMIT),
        name="gla",
    )(gqk, gqk, gv, gout, bcum, gnorm)


def _tail_kernel(x_ref, osb_ref, ogla_ref, gates_ref, wpa_ref, wpb_ref, wout_ref,
                 g_ref, w1_ref, w2_ref, gf_ref, o_ref, *, ff_chunk):
    d = x_ref.shape[-1]
    a = jnp.dot(osb_ref[0], wpa_ref[...], preferred_element_type=F32)
    c = jnp.dot(ogla_ref[0], wpb_ref[...], preferred_element_type=F32)
    ga = jax.nn.sigmoid(gates_ref[0, :, 0:d].astype(F32))
    gb = jax.nn.sigmoid(gates_ref[0, :, d:2 * d].astype(F32))
    mix = ga * a + gb * c
    x = x_ref[0] + jnp.dot(mix.astype(BF16), wout_ref[...], preferred_element_type=F32)

    h = _rms(x, g_ref[...]).astype(BF16)
    acc = jnp.zeros_like(x)
    for j in range(w1_ref.shape[1] // ff_chunk):
        u = jnp.dot(h, w1_ref[:, j * ff_chunk:(j + 1) * ff_chunk], preferred_element_type=F32)
        u = jnp.square(jnp.maximum(u, 0.0)).astype(BF16)
        acc = acc + jnp.dot(u, w2_ref[j * ff_chunk:(j + 1) * ff_chunk, :], preferred_element_type=F32)
    o_ref[0] = _rms(x + acc, gf_ref[...])


def _tail(x, osb, ogla, gates, wpa, wpb, wout, g, w1, w2, gf, tm, ff_chunk):
    b, s, d = x.shape
    tok = lambda width: pl.BlockSpec((1, tm, width), lambda bi, i: (bi, i, 0))
    vec = pl.BlockSpec((1, d), lambda bi, i: (0, 0))
    full = lambda w: pl.BlockSpec(w.shape, lambda bi, i: (0, 0), pipeline_mode=pl.Buffered(1))
    return pl.pallas_call(
        functools.partial(_tail_kernel, ff_chunk=ff_chunk),
        grid=(b, s // tm),
        in_specs=[tok(d), tok(SB_WIDTH), tok(GLA_VALUE_WIDTH), tok(2 * d), full(wpa), full(wpb),
                  full(wout), vec, full(w1), full(w2), vec],
        out_specs=tok(d),
        out_shape=jax.ShapeDtypeStruct((b, s, d), F32),
        compiler_params=pltpu.CompilerParams(
            dimension_semantics=("parallel", "parallel"), vmem_limit_bytes=VMEM_LIMIT),
        name="merge_mlp",
    )(x, osb, ogla, gates, wpa, wpb, wout, g, w1, w2, gf)


def _layer(x, norm_mix, w_in, w_gate_up, b_gate_up, gla_norm, w_proj_sb, w_proj_gla, w_out,
           norm_mlp, w_ff1, w_ff2, final_gain, tm):
    d = x.shape[-1]
    c_low = 3 * SB_WIDTH + 2 * GLA_KEY_WIDTH + 2 * GLA_VALUE_WIDTH
    w_main = w_in[:, :c_low].astype(BF16)
    w_gates = w_in[:, c_low + GLA_GATE_RANK:].astype(BF16)
    w_low = jnp.pad(w_in[:, c_low:c_low + GLA_GATE_RANK].astype(BF16),
                    ((0, 0), (0, LANES - GLA_GATE_RANK)))
    wup = jnp.concatenate(
        [w_gate_up, jnp.zeros((LANES - GLA_GATE_RANK, GLA_KEY_WIDTH), w_gate_up.dtype)], axis=0).astype(BF16)

    sbq, sbk, sbv, gqk, gv, gout, gates, bcum = _inproj(
        x, norm_mix.reshape(1, d), w_main, w_gates, w_low, wup, b_gate_up.reshape(1, -1), tm)
    o_sb = _sb_attention(sbq, sbk, sbv)
    o_gla = _gla(gqk, gv, gout, bcum, gla_norm.reshape(1, -1))
    return _tail(x, o_sb, o_gla, gates, w_proj_sb.astype(BF16), w_proj_gla.astype(BF16),
                 w_out.astype(BF16), norm_mlp.reshape(1, d), w_ff1.astype(BF16), w_ff2.astype(BF16),
                 final_gain, tm, ff_chunk=1024)


def kernel(x, norm_mix, w_in, w_gate_up, b_gate_up, gla_norm, w_proj_sb, w_proj_gla, w_out,
           norm_mlp, w_ff1, w_ff2, norm_final):
    depth = w_in.shape[0]
    s = x.shape[1]
    tm = min(512, s)
    assert depth == 1, "the final RMSNorm is fused into the last layer's MLP kernel"
    return _layer(x, norm_mix[0], w_in[0], w_gate_up[0], b_gate_up[0], gla_norm[0], w_proj_sb[0],
                  w_proj_gla[0], w_out[0], norm_mlp[0], w_ff1[0], w_ff2[0],
                  norm_final.reshape(1, -1), tm)
```

```python
import functools

import jax
import jax.numpy as jnp
from jax import lax
from jax.experimental import pallas as pl
from jax.experimental.pallas import tpu as pltpu

F32 = jnp.float32
BF16 = jnp.bfloat16
EPS = 1e-6

SB_HEADS = 8
SB_DIM = 64
SB_WIDTH = SB_HEADS * SB_DIM
GLA_HEADS = 4
GLA_DK = 128
GLA_DV = 256
GLA_KEY_WIDTH = GLA_HEADS * GLA_DK
GLA_VALUE_WIDTH = GLA_HEADS * GLA_DV
GLA_GATE_RANK = 16
GLA_GATE_TAU = 16.0
GLA_CHUNK = 64
GLA_SUB = 16
GLA_GROUP = 8
GLA_SAFE_SPREAD = 60.0
LANES = 128

SB_TQ = 128
SB_NEAR = 2
SB_FAR = 64
SB_QB = 4
SB_UNDERFLOW = -110.0

VMEM_LIMIT = 56 * 1024 * 1024

NT_DIMS = (((1,), (1,)), ((), ()))
TN_DIMS = (((0,), (0,)), ((), ()))


def _rms(x, g):
    ms = jnp.mean(x * x, axis=-1, keepdims=True)
    return (x * lax.rsqrt(ms + EPS)) * g


def _log_sigmoid(x):
    return -(jnp.maximum(-x, 0.0) + jnp.log(1.0 + jnp.exp(-jnp.abs(x))))


def _inproj_kernel(x_ref, g_ref, wa_ref, wg_ref, wl_ref, wup_ref, bup_ref, sbq_ref, sbk_ref, sbv_ref,
                   gqk_ref, gv_ref, gout_ref, gates_ref, bc_ref):
    h = _rms(x_ref[0], g_ref[...]).astype(BF16)

    def proj(w_ref, c0, width):
        return jnp.dot(h, w_ref[:, c0:c0 + width], preferred_element_type=F32)

    def sb_proj(idx, ref):
        r = proj(wa_ref, idx * SB_WIDTH, SB_WIDTH)
        if idx == 0:
            r = r * (SB_DIM ** -0.5)
        ref[0] = r.astype(BF16)

    low = proj(wl_ref, 0, LANES).astype(BF16)
    sb_proj(0, sbq_ref)
    gate = jnp.dot(low, wup_ref[...], preferred_element_type=F32) + bup_ref[...]
    sb_proj(1, sbk_ref)
    log_a = _log_sigmoid(gate) * (1.0 / GLA_GATE_TAU)
    hi = log_a.astype(BF16)
    r1 = log_a - hi.astype(F32)
    mid = r1.astype(BF16)
    lo = (r1 - mid.astype(F32)).astype(BF16)
    sb_proj(2, sbv_ref)
    c = 3 * SB_WIDTH
    gqk_ref[0] = proj(wa_ref, c, 2 * GLA_KEY_WIDTH).astype(BF16)
    c += 2 * GLA_KEY_WIDTH
    C = GLA_CHUNK
    row = lax.broadcasted_iota(jnp.int32, (C, 3 * C), 0)
    col = lax.broadcasted_iota(jnp.int32, (C, 3 * C), 1)
    ltri3 = ((col % C) <= row).astype(BF16)
    for c0 in range(0, log_a.shape[0], C):
        parts = jnp.concatenate([p[c0:c0 + C] for p in (hi, mid, lo)], axis=0)
        bc_ref[0, c0:c0 + C, :] = jnp.dot(ltri3, parts, preferred_element_type=F32)
    for ref in (gv_ref, gout_ref):
        ref[0] = proj(wa_ref, c, GLA_VALUE_WIDTH).astype(BF16)
        c += GLA_VALUE_WIDTH
    d_model = x_ref.shape[-1]
    for half in range(2):
        gates_ref[0, :, half * d_model:(half + 1) * d_model] = (
            proj(wg_ref, half * d_model, d_model).astype(BF16))


def _inproj(x, norm_g, w_main, w_gates, w_low, wup, bup, tm):
    b, s, d = x.shape
    assert tm % GLA_CHUNK == 0
    grid = (b, s // tm)
    tok = lambda width: pl.BlockSpec((1, tm, width), lambda bi, i: (bi, i, 0))
    out_shape = (
        jax.ShapeDtypeStruct((b, s, SB_WIDTH), BF16),
        jax.ShapeDtypeStruct((b, s, SB_WIDTH), BF16),
        jax.ShapeDtypeStruct((b, s, SB_WIDTH), BF16),
        jax.ShapeDtypeStruct((b, s, 2 * GLA_KEY_WIDTH), BF16),
        jax.ShapeDtypeStruct((b, s, GLA_VALUE_WIDTH), BF16),
        jax.ShapeDtypeStruct((b, s, GLA_VALUE_WIDTH), BF16),
        jax.ShapeDtypeStruct((b, s, 2 * d), BF16),
        jax.ShapeDtypeStruct((b, s, GLA_KEY_WIDTH), F32),
    )
    return pl.pallas_call(
        _inproj_kernel,
        grid=grid,
        in_specs=[
            tok(d),
            pl.BlockSpec((1, d), lambda bi, i: (0, 0)),
        ] + [pl.BlockSpec(w.shape, lambda bi, i: (0, 0), pipeline_mode=pl.Buffered(1))
             for w in (w_main, w_gates, w_low, wup, bup)],
        out_specs=(tok(SB_WIDTH), tok(SB_WIDTH), tok(SB_WIDTH), tok(2 * GLA_KEY_WIDTH), tok(GLA_VALUE_WIDTH),
                   tok(GLA_VALUE_WIDTH), tok(2 * d), tok(GLA_KEY_WIDTH)),
        out_shape=out_shape,
        compiler_params=pltpu.CompilerParams(
            dimension_semantics=("parallel", "parallel"), vmem_limit_bytes=VMEM_LIMIT),
        name="inproj",
    )(x, norm_g, w_main, w_gates, w_low, wup, bup)


def _aligned(start, tile):
    return start if isinstance(start, int) else pl.multiple_of(start, tile)


def _sb_tile(z, neg_tri2, mask):
    tk = z.shape[1]
    f = jnp.maximum(z, 0.0) + jnp.log(1.0 + jnp.exp(-jnp.abs(z)))
    if mask is not None:
        f = jnp.where(mask, f, 0.0)
    hi = f.astype(BF16)
    lo = (f - hi.astype(F32)).astype(BF16)
    cs = jnp.dot(jnp.concatenate([hi, lo], axis=1), neg_tri2, preferred_element_type=F32)
    return z + cs[:, :tk], cs[:, tk:]


def _sb_kernel(q_ref, k_ref, v_ref, o_ref, acc_ref, carry_ref):
    s_len = q_ref.shape[1]
    heads = range(q_ref.shape[2] // SB_DIM)
    T = SB_TQ
    nq = s_len // T
    lane_head = lax.broadcasted_iota(jnp.int32, (T, q_ref.shape[2]), 1) // SB_DIM
    row = lax.broadcasted_iota(jnp.int32, (T, T), 0)
    col = lax.broadcasted_iota(jnp.int32, (T, T), 1)
    neg_tri = jnp.where(row >= col, -1.0, 0.0).astype(BF16)
    half = jnp.concatenate([neg_tri, jnp.full((T, T), -1.0, BF16)], axis=1)
    neg_tri2 = jnp.concatenate([half, half], axis=0)
    past = col < row

    r2 = lax.broadcasted_iota(jnp.int32, (2 * T, 2 * T), 0) % T
    c2 = lax.broadcasted_iota(jnp.int32, (2 * T, 2 * T), 1)
    same_head = r2 // SB_FAR == (c2 % T) // SB_FAR
    far_tri2 = jnp.where(
        jnp.logical_and(same_head, jnp.logical_or(c2 >= T, r2 >= c2)), -1.0, 0.0).astype(BF16)
    lane_head_far = lax.broadcasted_iota(jnp.int32, (SB_FAR, q_ref.shape[2]), 1) // SB_DIM

    def windows(blocks, n_near, with_far):
        chains = [(u, hh) for u in range(len(blocks)) for hh in heads]
        q0s = [_aligned(i * T, T) for i in blocks]
        k0s = [_aligned((i - (n_near - 1)) * T, T) for i in blocks]
        f0s = [_aligned((i - (n_near - 1)) * T - SB_FAR, SB_FAR) for i in blocks]
        zs = []
        for u, hh in chains:
            q = head_only(q_ref[0, pl.ds(q0s[u], T), :], hh)
            zs.append(lax.dot_general(q, k_ref[0, pl.ds(k0s[u], n_near * T), :], NT_DIMS,
                                      preferred_element_type=F32))
        z_fars = []
        if with_far:
            for u in range(len(blocks)):
                k_far = k_ref[0, pl.ds(f0s[u], SB_FAR), :]
                keys = jnp.concatenate(
                    [jnp.where(lane_head_far == hh, k_far, jnp.zeros_like(k_far)) for hh in heads], axis=0)
                z_fars.append(lax.dot_general(q_ref[0, pl.ds(q0s[u], T), :], keys, NT_DIMS,
                                              preferred_element_type=F32))
        stats = []
        for z in zs:
            per_tile = []
            for jb in reversed(range(n_near)):
                mask = past if jb == n_near - 1 else None
                per_tile.append(_sb_tile(z[:, jb * T:(jb + 1) * T], neg_tri2, mask))
            stats.append(per_tile)
        far_stats = [_sb_tile(z, far_tri2, None) for z in z_fars]
        near = [([], []) for _ in blocks]
        for (u, hh), per_tile in zip(chains, stats):
            carry = jnp.zeros((T, T), F32)
            ws = []
            for n, (logw, total) in enumerate(per_tile):
                w = jnp.exp(logw + carry)
                if n == 0:
                    w = jnp.where(past, w, 0.0)
                ws.append(w.astype(BF16))
                carry = carry + total
            w_all = ws[0] if n_near == 1 else jnp.concatenate(ws[::-1], axis=1)
            near[u][0].append(jnp.dot(w_all, v_ref[0, pl.ds(k0s[u], n_near * T), :],
                                      preferred_element_type=F32))
            near[u][1].append(carry)
        out = []
        for u, (accs, carries) in enumerate(near):
            slab, carry = by_head(accs), by_head(carries)
            if with_far:
                logw, total = far_stats[u]
                v_far = v_ref[0, pl.ds(f0s[u], SB_FAR), :]
                values = jnp.concatenate(
                    [jnp.where(lane_head_far == hh, v_far, jnp.zeros_like(v_far)) for hh in heads], axis=0)
                slab = slab + jnp.dot(jnp.exp(logw + carry).astype(BF16), values,
                                      preferred_element_type=F32)
                carry = carry + total
            out.append((accs, carries, slab, carry))
        return out

    def largest(cs):
        m = cs[0]
        for c in cs[1:]:
            m = jnp.maximum(m, c)
        return jnp.max(m)

    def head_only(a, hh):
        return jnp.where(lane_head == hh, a, jnp.zeros_like(a))

    def by_head(per_head):
        out = per_head[0]
        for hh in list(heads)[1:]:
            out = jnp.where(lane_head == hh, per_head[hh], out)
        return out

    def store(i, slab):
        o_ref[0, pl.ds(_aligned(i * T, T), T), :] = slab.astype(BF16)

    def finish(i, jb_first, accs, carries, cmax):
        q0 = _aligned(i * T, T)
        for n in range(len(accs)):
            acc_ref[n] = accs[n]
            carry_ref[n] = carries[n]

        def cond(st):
            jb, cmax = st
            return jnp.logical_and(jb >= 0, cmax > SB_UNDERFLOW)

        def body(st):
            jb, _ = st
            k0 = pl.multiple_of(jb * T, T)
            for n, hh in enumerate(heads):
                q = head_only(q_ref[0, pl.ds(q0, T), :], hh)
                z = lax.dot_general(q, k_ref[0, pl.ds(k0, T), :], NT_DIMS,
                                    preferred_element_type=F32)
                logw, total = _sb_tile(z, neg_tri2, None)
                w = jnp.exp(logw + carry_ref[n]).astype(BF16)
                acc_ref[n] += jnp.dot(w, v_ref[0, pl.ds(k0, T), :],
                                      preferred_element_type=F32)
                carry_ref[n] += total
            return jb - 1, largest([carry_ref[n] for n in range(len(accs))])

        lax.while_loop(cond, body, (jb_first, cmax))
        store(i, by_head([acc_ref[n] for n in range(len(accs))]))

    def run(blocks, n_near, with_far=False):
        results = windows(blocks, n_near, with_far)
        for i, (_, _, slab, _) in zip(blocks, results):
            store(i, slab)
        firsts = [i - n_near for i in blocks]
        if all(isinstance(jb, int) and jb < 0 for jb in firsts):
            return
        cmaxs = [jnp.max(carry) for _, _, _, carry in results]
        unfinished = [jnp.logical_and(jb >= 0, cmax > SB_UNDERFLOW) for jb, cmax in zip(firsts, cmaxs)]

        @pl.when(functools.reduce(jnp.logical_or, unfinished))
        def _():
            for i, jb, (accs, carries, _, _), cmax, more in zip(blocks, firsts, results, cmaxs, unfinished):
                @pl.when(more)
                def _(i=i, jb=jb, accs=accs, carries=carries, cmax=cmax):
                    finish(i, jnp.int32(jb), accs, carries, cmax)

    n_near = min(SB_NEAR, nq)
    for i in range(n_near - 1):
        run([i], i + 1)
    if nq > n_near - 1:
        run([n_near - 1], n_near)
    n_lead = n_near + (nq - n_near) % SB_QB
    if n_lead > n_near:
        run(list(range(n_near, n_lead)), n_near, True)

    def q_body(p, _):
        run([n_lead + p * SB_QB + u for u in range(SB_QB)], n_near, True)
        return 0

    lax.fori_loop(0, (nq - n_lead) // SB_QB, q_body, 0)


def _sb_attention(q, k, v):
    b, s, _ = q.shape
    hp = LANES // SB_DIM
    spec = pl.BlockSpec((1, s, LANES), lambda bi, h: (bi, 0, h))
    return pl.pallas_call(
        _sb_kernel,
        grid=(b, SB_HEADS // hp),
        in_specs=[spec, spec, spec],
        out_specs=spec,
        out_shape=jax.ShapeDtypeStruct((b, s, SB_WIDTH), BF16),
        scratch_shapes=[pltpu.VMEM((hp, SB_TQ, LANES), F32),
                        pltpu.VMEM((hp, SB_TQ, SB_TQ), F32)],
        compiler_params=pltpu.CompilerParams(
            dimension_semantics=("parallel", "parallel"), vmem_limit_bytes=VMEM_LIMIT),
        name="sb_attention",
    )(q, k, v)


def _rows_bcast(a, offset):
    n_sub = a.shape[0] // GLA_SUB
    return jnp.concatenate(
        [jnp.broadcast_to(a[i * GLA_SUB + offset:i * GLA_SUB + offset + 1, :], (GLA_SUB, a.shape[1]))
         for i in range(n_sub)], axis=0)


def _gla_kernel(q_ref, k_ref, v_ref, go_ref, bc_ref, gn_ref, o_ref, st_ref, safe_ref):
    s_len = q_ref.shape[1]
    C = GLA_CHUNK
    G = GLA_GROUP
    n_sub = C // GLA_SUB
    scale = GLA_DK ** -0.5

    r_cc = lax.broadcasted_iota(jnp.int32, (C, C), 0)
    c_cc = lax.broadcasted_iota(jnp.int32, (C, C), 1)
    causal = c_cc <= r_cc
    diag_keep = jnp.logical_and((r_cc // GLA_SUB) == (c_cc // GLA_SUB), causal)
    r_loc = lax.broadcasted_iota(jnp.int32, (C, GLA_DK), 0) % GLA_SUB
    row_pick = [(r_loc == s).astype(BF16) for s in range(GLA_SUB)]

    n_groups = s_len // (C * G)
    n_chunks = s_len // C
    spread = (bc_ref[0, pl.ds(0, n_chunks, stride=C), :]
              - bc_ref[0, pl.ds(C - 1, n_chunks, stride=C), :])
    for g in range(n_groups):
        safe_ref[g] = (jnp.max(spread[g * G:(g + 1) * G]) < GLA_SAFE_SPREAD).astype(jnp.int32)

    def scores_factored(qs, ks, bcums):
        out = []
        for q, k, bcum in zip(qs, ks, bcums):
            r0 = bcum[0:1, :]
            qn = (q * jnp.exp(bcum - r0)).astype(BF16)
            kn = (k * jnp.exp(r0 - bcum)).astype(BF16)
            sc = lax.dot_general(qn, kn, NT_DIMS, preferred_element_type=F32)
            out.append(jnp.where(causal, sc, 0.0).astype(BF16))
        return out

    def scores_bounded(qs, ks, t0s, bcums):
        offs = []
        for q, k, bcum in zip(qs, ks, bcums):
            qn = (q * jnp.exp(bcum - _rows_bcast(bcum, 0))).astype(BF16)
            off = [jnp.zeros((GLA_SUB, C), F32)]
            for i in range(1, n_sub):
                n_prev = i * GLA_SUB
                ref_row = bcum[n_prev:n_prev + 1, :]
                kn = jnp.concatenate(
                    [(k[:n_prev] * jnp.exp(ref_row - bcum[:n_prev])).astype(BF16),
                     jnp.zeros((C - n_prev, GLA_DK), BF16)], axis=0)
                off.append(lax.dot_general(qn[i * GLA_SUB:(i + 1) * GLA_SUB], kn, NT_DIMS,
                                           preferred_element_type=F32))
            offs.append(jnp.concatenate(off, axis=0))
        out = []
        for q, t0, bcum, off in zip(qs, t0s, bcums, offs):
            kb = k_ref[0, pl.ds(t0, C), :]
            prods, keys = [], []
            for s in range(GLA_SUB):
                decay = jnp.exp(jnp.minimum(bcum - _rows_bcast(bcum, s), 0.0))
                prods.append((q * decay).astype(BF16))
                keys.append(kb * row_pick[s])
            diag = lax.dot_general(jnp.concatenate(prods, axis=1), jnp.concatenate(keys, axis=1),
                                   NT_DIMS, preferred_element_type=F32)
            out.append(jnp.where(diag_keep, diag, off).astype(BF16))
        return out

    def group(gi, factored):
        t0s = [pl.multiple_of((gi * G + u) * C, C) for u in range(G)]
        qs = [q_ref[0, pl.ds(t0, C), :].astype(F32) * scale for t0 in t0s]
        ks = [k_ref[0, pl.ds(t0, C), :].astype(F32) for t0 in t0s]
        vs = [v_ref[0, pl.ds(t0, C), :] for t0 in t0s]
        bcums = [bc_ref[0, pl.ds(t0, C), :] for t0 in t0s]
        scores = scores_factored(qs, ks, bcums) if factored else scores_bounded(qs, ks, t0s, bcums)

        kvs, q_ins, decays = [], [], []
        for q, k, v, bcum in zip(qs, ks, vs, bcums):
            b_last = bcum[C - 1:C, :]
            k_dec = (k * jnp.exp(b_last - bcum)).astype(BF16)
            kvs.append(lax.dot_general(v, k_dec, TN_DIMS, preferred_element_type=F32))
            q_ins.append((q * jnp.exp(bcum)).astype(BF16))
            decays.append(jnp.exp(b_last))
        st = st_ref[...]
        outs = []
        for q_in, kv, decay in zip(q_ins, kvs, decays):
            outs.append(lax.dot_general(q_in, st.astype(BF16), NT_DIMS, preferred_element_type=F32))
            st = st * decay + kv
        st_ref[...] = st

        for t0, o, sc, v in zip(t0s, outs, scores, vs):
            o = o + jnp.dot(sc, v, preferred_element_type=F32)
            y = _rms(o, gn_ref[...])
            g = go_ref[0, pl.ds(t0, C), :].astype(F32)
            o_ref[0, pl.ds(t0, C), :] = (y * (g * jax.nn.sigmoid(g))).astype(BF16)

    def main(gi, _):
        lax.cond(safe_ref[gi] == 1, lambda: group(gi, True), lambda: group(gi, False))
        return 0

    st_ref[...] = jnp.zeros_like(st_ref)
    lax.fori_loop(0, n_groups, main, 0)


def _gla(gqk, gv, gout, bcum, gnorm):
    b, s, _ = gv.shape
    n_groups, rem = divmod(s, GLA_CHUNK * GLA_GROUP)
    assert rem == 0, s
    return pl.pallas_call(
        _gla_kernel,
        grid=(b, GLA_HEADS),
        in_specs=[
            pl.BlockSpec((1, s, GLA_DK), lambda bi, h: (bi, 0, h)),
            pl.BlockSpec((1, s, GLA_DK), lambda bi, h: (bi, 0, GLA_HEADS + h)),
            pl.BlockSpec((1, s, GLA_DV), lambda bi, h: (bi, 0, h)),
            pl.BlockSpec((1, s, GLA_DV), lambda bi, h: (bi, 0, h)),
            pl.BlockSpec((1, s, GLA_DK), lambda bi, h: (bi, 0, h)),
            pl.BlockSpec((1, GLA_DV), lambda bi, h: (0, 0)),
        ],
        out_specs=pl.BlockSpec((1, s, GLA_DV), lambda bi, h: (bi, 0, h)),
        out_shape=jax.ShapeDtypeStruct((b, s, GLA_VALUE_WIDTH), BF16),
        scratch_shapes=[pltpu.VMEM((GLA_DV, GLA_DK), F32),
                        pltpu.SMEM((n_groups,), jnp.int32)],
        compiler_params=pltpu.CompilerParams(
            dimension_semantics=("parallel", "parallel"), vmem_limit_bytes=VMEM_LIMIT),
        name="gla",
    )(gqk, gqk, gv, gout, bcum, gnorm)


def _tail_kernel(x_ref, osb_ref, ogla_ref, gates_ref, wpa_ref, wpb_ref, wout_ref,
                 g_ref, w1_ref, w2_ref, gf_ref, o_ref, *, ff_chunk):
    d = x_ref.shape[-1]
    a = jnp.dot(osb_ref[0], wpa_ref[...], preferred_element_type=F32)
    c = jnp.dot(ogla_ref[0], wpb_ref[...], preferred_element_type=F32)
    ga = jax.nn.sigmoid(gates_ref[0, :, 0:d].astype(F32))
    gb = jax.nn.sigmoid(gates_ref[0, :, d:2 * d].astype(F32))
    mix = ga * a + gb * c
    x = x_ref[0] + jnp.dot(mix.astype(BF16), wout_ref[...], preferred_element_type=F32)

    h = _rms(x, g_ref[...]).astype(BF16)
    acc = jnp.zeros_like(x)
    for j in range(w1_ref.shape[1] // ff_chunk):
        u = jnp.dot(h, w1_ref[:, j * ff_chunk:(j + 1) * ff_chunk], preferred_element_type=F32)
        u = jnp.square(jnp.maximum(u, 0.0)).astype(BF16)
        acc = acc + jnp.dot(u, w2_ref[j * ff_chunk:(j + 1) * ff_chunk, :], preferred_element_type=F32)
    o_ref[0] = _rms(x + acc, gf_ref[...])


def _tail(x, osb, ogla, gates, wpa, wpb, wout, g, w1, w2, gf, tm, ff_chunk):
    b, s, d = x.shape
    tok = lambda width: pl.BlockSpec((1, tm, width), lambda bi, i: (bi, i, 0))
    vec = pl.BlockSpec((1, d), lambda bi, i: (0, 0))
    full = lambda w: pl.BlockSpec(w.shape, lambda bi, i: (0, 0), pipeline_mode=pl.Buffered(1))
    return pl.pallas_call(
        functools.partial(_tail_kernel, ff_chunk=ff_chunk),
        grid=(b, s // tm),
        in_specs=[tok(d), tok(SB_WIDTH), tok(GLA_VALUE_WIDTH), tok(2 * d), full(wpa), full(wpb),
                  full(wout), vec, full(w1), full(w2), vec],
        out_specs=tok(d),
        out_shape=jax.ShapeDtypeStruct((b, s, d), F32),
        compiler_params=pltpu.CompilerParams(
            dimension_semantics=("parallel", "parallel"), vmem_limit_bytes=VMEM_LIMIT),
        name="merge_mlp",
    )(x, osb, ogla, gates, wpa, wpb, wout, g, w1, w2, gf)


def _layer(x, norm_mix, w_in, w_gate_up, b_gate_up, gla_norm, w_proj_sb, w_proj_gla, w_out,
           norm_mlp, w_ff1, w_ff2, final_gain, tm):
    d = x.shape[-1]
    c_low = 3 * SB_WIDTH + 2 * GLA_KEY_WIDTH + 2 * GLA_VALUE_WIDTH
    w_main = w_in[:, :c_low].astype(BF16)
    w_gates = w_in[:, c_low + GLA_GATE_RANK:].astype(BF16)
    w_low = jnp.pad(w_in[:, c_low:c_low + GLA_GATE_RANK].astype(BF16),
                    ((0, 0), (0, LANES - GLA_GATE_RANK)))
    wup = jnp.concatenate(
        [w_gate_up, jnp.zeros((LANES - GLA_GATE_RANK, GLA_KEY_WIDTH), w_gate_up.dtype)], axis=0).astype(BF16)

    sbq, sbk, sbv, gqk, gv, gout, gates, bcum = _inproj(
        x, norm_mix.reshape(1, d), w_main, w_gates, w_low, wup, b_gate_up.reshape(1, -1), tm)
    o_sb = _sb_attention(sbq, sbk, sbv)
    o_gla = _gla(gqk, gv, gout, bcum, gla_norm.reshape(1, -1))
    return _tail(x, o_sb, o_gla, gates, w_proj_sb.astype(BF16), w_proj_gla.astype(BF16),
                 w_out.astype(BF16), norm_mlp.reshape(1, d), w_ff1.astype(BF16), w_ff2.astype(BF16),
                 final_gain, tm, ff_chunk=1024)


def kernel(x, norm_mix, w_in, w_gate_up, b_gate_up, gla_norm, w_proj_sb, w_proj_gla, w_out,
           norm_mlp, w_ff1, w_ff2, norm_final):
    depth = w_in.shape[0]
    s = x.shape[1]
    tm = min(512, s)
    assert depth == 1, "the final RMSNorm is fused into the last layer's MLP kernel"
    return _layer(x, norm_mix[0], w_in[0], w_gate_up[0], b_gate_up[0], gla_norm[0], w_proj_sb[0],
                  w_proj_gla[0], w_out[0], norm_mlp[0], w_ff1[0], w_ff2[0],
                  norm_final.reshape(1, -1), tm)
```

```python
import functools

import jax
import jax.numpy as jnp
from jax import lax
from jax.experimental import pallas as pl
from jax.experimental.pallas import tpu as pltpu

F32 = jnp.float32
BF16 = jnp.bfloat16
EPS = 1e-6

SB_HEADS = 8
SB_DIM = 64
SB_WIDTH = SB_HEADS * SB_DIM
GLA_HEADS = 4
GLA_DK = 128
GLA_DV = 256
GLA_KEY_WIDTH = GLA_HEADS * GLA_DK
GLA_VALUE_WIDTH = GLA_HEADS * GLA_DV
GLA_GATE_RANK = 16
GLA_GATE_TAU = 16.0
GLA_CHUNK = 128
GLA_SUB = 16
GLA_GROUP = 4
GLA_SAFE_SPREAD = 60.0
LANES = 128

SB_TQ = 128
SB_NEAR = 2
SB_FAR = 64
SB_QB = 4
SB_UNDERFLOW = -110.0

VMEM_LIMIT = 56 * 1024 * 1024

NT_DIMS = (((1,), (1,)), ((), ()))
TN_DIMS = (((0,), (0,)), ((), ()))


def _rms(x, g):
    ms = jnp.mean(x * x, axis=-1, keepdims=True)
    return (x * lax.rsqrt(ms + EPS)) * g


def _log_sigmoid(x):
    return -(jnp.maximum(-x, 0.0) + jnp.log(1.0 + jnp.exp(-jnp.abs(x))))


def _inproj_kernel(x_ref, g_ref, wa_ref, wg_ref, wl_ref, wup_ref, bup_ref, sbq_ref, sbk_ref, sbv_ref,
                   gqk_ref, gv_ref, gout_ref, gates_ref, bc_ref):
    h = _rms(x_ref[0], g_ref[...]).astype(BF16)

    def proj(w_ref, c0, width):
        return jnp.dot(h, w_ref[:, c0:c0 + width], preferred_element_type=F32)

    def sb_proj(idx, ref):
        r = proj(wa_ref, idx * SB_WIDTH, SB_WIDTH)
        if idx == 0:
            r = r * (SB_DIM ** -0.5)
        ref[0] = r.astype(BF16)

    low = proj(wl_ref, 0, LANES).astype(BF16)
    sb_proj(0, sbq_ref)
    gate = jnp.dot(low, wup_ref[...], preferred_element_type=F32) + bup_ref[...]
    sb_proj(1, sbk_ref)
    log_a = _log_sigmoid(gate) * (1.0 / GLA_GATE_TAU)
    hi = log_a.astype(BF16)
    r1 = log_a - hi.astype(F32)
    mid = r1.astype(BF16)
    lo = (r1 - mid.astype(F32)).astype(BF16)
    sb_proj(2, sbv_ref)
    c = 3 * SB_WIDTH
    gqk_ref[0] = proj(wa_ref, c, 2 * GLA_KEY_WIDTH).astype(BF16)
    c += 2 * GLA_KEY_WIDTH
    C = GLA_CHUNK
    row = lax.broadcasted_iota(jnp.int32, (C, 3 * C), 0)
    col = lax.broadcasted_iota(jnp.int32, (C, 3 * C), 1)
    ltri3 = ((col % C) <= row).astype(BF16)
    for c0 in range(0, log_a.shape[0], C):
        parts = jnp.concatenate([p[c0:c0 + C] for p in (hi, mid, lo)], axis=0)
        bc_ref[0, c0:c0 + C, :] = jnp.dot(ltri3, parts, preferred_element_type=F32)
    for ref in (gv_ref, gout_ref):
        ref[0] = proj(wa_ref, c, GLA_VALUE_WIDTH).astype(BF16)
        c += GLA_VALUE_WIDTH
    d_model = x_ref.shape[-1]
    for half in range(2):
        gates_ref[0, :, half * d_model:(half + 1) * d_model] = (
            proj(wg_ref, half * d_model, d_model).astype(BF16))


def _inproj(x, norm_g, w_main, w_gates, w_low, wup, bup, tm):
    b, s, d = x.shape
    assert tm % GLA_CHUNK == 0
    grid = (b, s // tm)
    tok = lambda width: pl.BlockSpec((1, tm, width), lambda bi, i: (bi, i, 0))
    out_shape = (
        jax.ShapeDtypeStruct((b, s, SB_WIDTH), BF16),
        jax.ShapeDtypeStruct((b, s, SB_WIDTH), BF16),
        jax.ShapeDtypeStruct((b, s, SB_WIDTH), BF16),
        jax.ShapeDtypeStruct((b, s, 2 * GLA_KEY_WIDTH), BF16),
        jax.ShapeDtypeStruct((b, s, GLA_VALUE_WIDTH), BF16),
        jax.ShapeDtypeStruct((b, s, GLA_VALUE_WIDTH), BF16),
        jax.ShapeDtypeStruct((b, s, 2 * d), BF16),
        jax.ShapeDtypeStruct((b, s, GLA_KEY_WIDTH), F32),
    )
    return pl.pallas_call(
        _inproj_kernel,
        grid=grid,
        in_specs=[
            tok(d),
            pl.BlockSpec((1, d), lambda bi, i: (0, 0)),
        ] + [pl.BlockSpec(w.shape, lambda bi, i: (0, 0), pipeline_mode=pl.Buffered(1))
             for w in (w_main, w_gates, w_low, wup, bup)],
        out_specs=(tok(SB_WIDTH), tok(SB_WIDTH), tok(SB_WIDTH), tok(2 * GLA_KEY_WIDTH), tok(GLA_VALUE_WIDTH),
                   tok(GLA_VALUE_WIDTH), tok(2 * d), tok(GLA_KEY_WIDTH)),
        out_shape=out_shape,
        compiler_params=pltpu.CompilerParams(
            dimension_semantics=("parallel", "parallel"), vmem_limit_bytes=VMEM_LIMIT),
        name="inproj",
    )(x, norm_g, w_main, w_gates, w_low, wup, bup)


def _aligned(start, tile):
    return start if isinstance(start, int) else pl.multiple_of(start, tile)


def _sb_tile(z, neg_tri2, mask):
    tk = z.shape[1]
    f = jnp.maximum(z, 0.0) + jnp.log(1.0 + jnp.exp(-jnp.abs(z)))
    if mask is not None:
        f = jnp.where(mask, f, 0.0)
    hi = f.astype(BF16)
    lo = (f - hi.astype(F32)).astype(BF16)
    cs = jnp.dot(jnp.concatenate([hi, lo], axis=1), neg_tri2, preferred_element_type=F32)
    return z + cs[:, :tk], cs[:, tk:]


def _sb_kernel(q_ref, k_ref, v_ref, o_ref, acc_ref, carry_ref):
    s_len = q_ref.shape[1]
    heads = range(q_ref.shape[2] // SB_DIM)
    T = SB_TQ
    nq = s_len // T
    lane_head = lax.broadcasted_iota(jnp.int32, (T, q_ref.shape[2]), 1) // SB_DIM
    row = lax.broadcasted_iota(jnp.int32, (T, T), 0)
    col = lax.broadcasted_iota(jnp.int32, (T, T), 1)
    neg_tri = jnp.where(row >= col, -1.0, 0.0).astype(BF16)
    half = jnp.concatenate([neg_tri, jnp.full((T, T), -1.0, BF16)], axis=1)
    neg_tri2 = jnp.concatenate([half, half], axis=0)
    past = col < row

    r2 = lax.broadcasted_iota(jnp.int32, (2 * T, 2 * T), 0) % T
    c2 = lax.broadcasted_iota(jnp.int32, (2 * T, 2 * T), 1)
    same_head = r2 // SB_FAR == (c2 % T) // SB_FAR
    far_tri2 = jnp.where(
        jnp.logical_and(same_head, jnp.logical_or(c2 >= T, r2 >= c2)), -1.0, 0.0).astype(BF16)
    lane_head_far = lax.broadcasted_iota(jnp.int32, (SB_FAR, q_ref.shape[2]), 1) // SB_DIM

    def windows(blocks, n_near, with_far):
        chains = [(u, hh) for u in range(len(blocks)) for hh in heads]
        q0s = [_aligned(i * T, T) for i in blocks]
        k0s = [_aligned((i - (n_near - 1)) * T, T) for i in blocks]
        f0s = [_aligned((i - (n_near - 1)) * T - SB_FAR, SB_FAR) for i in blocks]
        zs = []
        for u, hh in chains:
            q = head_only(q_ref[0, pl.ds(q0s[u], T), :], hh)
            zs.append(lax.dot_general(q, k_ref[0, pl.ds(k0s[u], n_near * T), :], NT_DIMS,
                                      preferred_element_type=F32))
        z_fars = []
        if with_far:
            for u in range(len(blocks)):
                k_far = k_ref[0, pl.ds(f0s[u], SB_FAR), :]
                keys = jnp.concatenate(
                    [jnp.where(lane_head_far == hh, k_far, jnp.zeros_like(k_far)) for hh in heads], axis=0)
                z_fars.append(lax.dot_general(q_ref[0, pl.ds(q0s[u], T), :], keys, NT_DIMS,
                                              preferred_element_type=F32))
        stats = []
        for z in zs:
            per_tile = []
            for jb in reversed(range(n_near)):
                mask = past if jb == n_near - 1 else None
                per_tile.append(_sb_tile(z[:, jb * T:(jb + 1) * T], neg_tri2, mask))
            stats.append(per_tile)
        far_stats = [_sb_tile(z, far_tri2, None) for z in z_fars]
        near = [([], []) for _ in blocks]
        for (u, hh), per_tile in zip(chains, stats):
            carry = jnp.zeros((T, T), F32)
            ws = []
            for n, (logw, total) in enumerate(per_tile):
                w = jnp.exp(logw + carry)
                if n == 0:
                    w = jnp.where(past, w, 0.0)
                ws.append(w.astype(BF16))
                carry = carry + total
            w_all = ws[0] if n_near == 1 else jnp.concatenate(ws[::-1], axis=1)
            near[u][0].append(jnp.dot(w_all, v_ref[0, pl.ds(k0s[u], n_near * T), :],
                                      preferred_element_type=F32))
            near[u][1].append(carry)
        out = []
        for u, (accs, carries) in enumerate(near):
            slab, carry = by_head(accs), by_head(carries)
            if with_far:
                logw, total = far_stats[u]
                v_far = v_ref[0, pl.ds(f0s[u], SB_FAR), :]
                values = jnp.concatenate(
                    [jnp.where(lane_head_far == hh, v_far, jnp.zeros_like(v_far)) for hh in heads], axis=0)
                slab = slab + jnp.dot(jnp.exp(logw + carry).astype(BF16), values,
                                      preferred_element_type=F32)
                carry = carry + total
            out.append((accs, carries, slab, carry))
        return out

    def largest(cs):
        m = cs[0]
        for c in cs[1:]:
            m = jnp.maximum(m, c)
        return jnp.max(m)

    def head_only(a, hh):
        return jnp.where(lane_head == hh, a, jnp.zeros_like(a))

    def by_head(per_head):
        out = per_head[0]
        for hh in list(heads)[1:]:
            out = jnp.where(lane_head == hh, per_head[hh], out)
        return out

    def store(i, slab):
        o_ref[0, pl.ds(_aligned(i * T, T), T), :] = slab.astype(BF16)

    def finish(i, jb_first, accs, carries, cmax):
        q0 = _aligned(i * T, T)
        for n in range(len(accs)):
            acc_ref[n] = accs[n]
            carry_ref[n] = carries[n]

        def cond(st):
            jb, cmax = st
            return jnp.logical_and(jb >= 0, cmax > SB_UNDERFLOW)

        def body(st):
            jb, _ = st
            k0 = pl.multiple_of(jb * T, T)
            for n, hh in enumerate(heads):
                q = head_only(q_ref[0, pl.ds(q0, T), :], hh)
                z = lax.dot_general(q, k_ref[0, pl.ds(k0, T), :], NT_DIMS,
                                    preferred_element_type=F32)
                logw, total = _sb_tile(z, neg_tri2, None)
                w = jnp.exp(logw + carry_ref[n]).astype(BF16)
                acc_ref[n] += jnp.dot(w, v_ref[0, pl.ds(k0, T), :],
                                      preferred_element_type=F32)
                carry_ref[n] += total
            return jb - 1, largest([carry_ref[n] for n in range(len(accs))])

        lax.while_loop(cond, body, (jb_first, cmax))
        store(i, by_head([acc_ref[n] for n in range(len(accs))]))

    def run(blocks, n_near, with_far=False):
        results = windows(blocks, n_near, with_far)
        for i, (_, _, slab, _) in zip(blocks, results):
            store(i, slab)
        firsts = [i - n_near for i in blocks]
        if all(isinstance(jb, int) and jb < 0 for jb in firsts):
            return
        cmaxs = [jnp.max(carry) for _, _, _, carry in results]
        unfinished = [jnp.logical_and(jb >= 0, cmax > SB_UNDERFLOW) for jb, cmax in zip(firsts, cmaxs)]

        @pl.when(functools.reduce(jnp.logical_or, unfinished))
        def _():
            for i, jb, (accs, carries, _, _), cmax, more in zip(blocks, firsts, results, cmaxs, unfinished):
                @pl.when(more)
                def _(i=i, jb=jb, accs=accs, carries=carries, cmax=cmax):
                    finish(i, jnp.int32(jb), accs, carries, cmax)

    n_near = min(SB_NEAR, nq)
    for i in range(n_near - 1):
        run([i], i + 1)
    if nq > n_near - 1:
        run([n_near - 1], n_near)
    n_lead = n_near + (nq - n_near) % SB_QB
    if n_lead > n_near:
        run(list(range(n_near, n_lead)), n_near, True)

    def q_body(p, _):
        run([n_lead + p * SB_QB + u for u in range(SB_QB)], n_near, True)
        return 0

    lax.fori_loop(0, (nq - n_lead) // SB_QB, q_body, 0)


def _sb_attention(q, k, v):
    b, s, _ = q.shape
    hp = LANES // SB_DIM
    spec = pl.BlockSpec((1, s, LANES), lambda bi, h: (bi, 0, h))
    return pl.pallas_call(
        _sb_kernel,
        grid=(b, SB_HEADS // hp),
        in_specs=[spec, spec, spec],
        out_specs=spec,
        out_shape=jax.ShapeDtypeStruct((b, s, SB_WIDTH), BF16),
        scratch_shapes=[pltpu.VMEM((hp, SB_TQ, LANES), F32),
                        pltpu.VMEM((hp, SB_TQ, SB_TQ), F32)],
        compiler_params=pltpu.CompilerParams(
            dimension_semantics=("parallel", "parallel"), vmem_limit_bytes=VMEM_LIMIT),
        name="sb_attention",
    )(q, k, v)


def _rows_bcast(a, offset):
    n_sub = a.shape[0] // GLA_SUB
    return jnp.concatenate(
        [jnp.broadcast_to(a[i * GLA_SUB + offset:i * GLA_SUB + offset + 1, :], (GLA_SUB, a.shape[1]))
         for i in range(n_sub)], axis=0)


def _gla_kernel(q_ref, k_ref, v_ref, go_ref, bc_ref, gn_ref, o_ref, st_ref, raw_ref, safe_ref):
    s_len = q_ref.shape[1]
    C = GLA_CHUNK
    G = GLA_GROUP
    n_sub = C // GLA_SUB
    scale = GLA_DK ** -0.5

    r_cc = lax.broadcasted_iota(jnp.int32, (C, C), 0)
    c_cc = lax.broadcasted_iota(jnp.int32, (C, C), 1)
    causal = c_cc <= r_cc
    diag_keep = jnp.logical_and((r_cc // GLA_SUB) == (c_cc // GLA_SUB), causal)
    r_loc = lax.broadcasted_iota(jnp.int32, (C, GLA_DK), 0) % GLA_SUB
    row_pick = [(r_loc == s).astype(BF16) for s in range(GLA_SUB)]

    n_groups = s_len // (C * G)
    n_chunks = s_len // C
    spread = (bc_ref[0, pl.ds(0, n_chunks, stride=C), :]
              - bc_ref[0, pl.ds(C - 1, n_chunks, stride=C), :])
    for g in range(n_groups):
        safe_ref[g] = (jnp.max(spread[g * G:(g + 1) * G]) < GLA_SAFE_SPREAD).astype(jnp.int32)

    def scores_factored(qs, ks, bcums):
        out = []
        for q, k, bcum in zip(qs, ks, bcums):
            r0 = bcum[0:1, :]
            qn = (q * jnp.exp(bcum - r0)).astype(BF16)
            kn = (k * jnp.exp(r0 - bcum)).astype(BF16)
            sc = lax.dot_general(qn, kn, NT_DIMS, preferred_element_type=F32)
            out.append(jnp.where(causal, sc, 0.0).astype(BF16))
        return out

    def scores_bounded(qs, ks, t0s, bcums):
        offs = []
        for q, k, bcum in zip(qs, ks, bcums):
            qn = (q * jnp.exp(bcum - _rows_bcast(bcum, 0))).astype(BF16)
            off = [jnp.zeros((GLA_SUB, C), F32)]
            for i in range(1, n_sub):
                n_prev = i * GLA_SUB
                ref_row = bcum[n_prev:n_prev + 1, :]
                kn = jnp.concatenate(
                    [(k[:n_prev] * jnp.exp(ref_row - bcum[:n_prev])).astype(BF16),
                     jnp.zeros((C - n_prev, GLA_DK), BF16)], axis=0)
                off.append(lax.dot_general(qn[i * GLA_SUB:(i + 1) * GLA_SUB], kn, NT_DIMS,
                                           preferred_element_type=F32))
            offs.append(jnp.concatenate(off, axis=0))
        out = []
        for q, t0, bcum, off in zip(qs, t0s, bcums, offs):
            kb = k_ref[0, pl.ds(t0, C), :]
            prods, keys = [], []
            for s in range(GLA_SUB):
                decay = jnp.exp(jnp.minimum(bcum - _rows_bcast(bcum, s), 0.0))
                prods.append((q * decay).astype(BF16))
                keys.append(kb * row_pick[s])
            diag = lax.dot_general(jnp.concatenate(prods, axis=1), jnp.concatenate(keys, axis=1),
                                   NT_DIMS, preferred_element_type=F32)
            out.append(jnp.where(diag_keep, diag, off).astype(BF16))
        return out

    def finalize(gi):
        slot = gi % 2
        for u in range(G):
            t0 = pl.multiple_of((gi * G + u) * C, C)
            y = _rms(raw_ref[slot, u * C:(u + 1) * C, :], gn_ref[...])
            g = go_ref[0, pl.ds(t0, C), :].astype(F32)
            o_ref[0, pl.ds(t0, C), :] = (y * (g * jax.nn.sigmoid(g))).astype(BF16)

    def group(gi, factored, finalize_previous):
        if finalize_previous:
            finalize(gi - 1)
        t0s = [pl.multiple_of((gi * G + u) * C, C) for u in range(G)]
        qs = [q_ref[0, pl.ds(t0, C), :].astype(F32) * scale for t0 in t0s]
        ks = [k_ref[0, pl.ds(t0, C), :].astype(F32) for t0 in t0s]
        vs = [v_ref[0, pl.ds(t0, C), :] for t0 in t0s]
        bcums = [bc_ref[0, pl.ds(t0, C), :] for t0 in t0s]
        scores = scores_factored(qs, ks, bcums) if factored else scores_bounded(qs, ks, t0s, bcums)

        kvs, q_ins, decays = [], [], []
        for q, k, v, bcum in zip(qs, ks, vs, bcums):
            b_last = bcum[C - 1:C, :]
            k_dec = (k * jnp.exp(b_last - bcum)).astype(BF16)
            kvs.append(lax.dot_general(v, k_dec, TN_DIMS, preferred_element_type=F32))
            q_ins.append((q * jnp.exp(bcum)).astype(BF16))
            decays.append(jnp.exp(b_last))
        st = st_ref[...]
        outs = []
        for q_in, kv, decay in zip(q_ins, kvs, decays):
            outs.append(lax.dot_general(q_in, st.astype(BF16), NT_DIMS, preferred_element_type=F32))
            st = st * decay + kv
        st_ref[...] = st

        slot = gi % 2
        for u, (o, sc, v) in enumerate(zip(outs, scores, vs)):
            raw_ref[slot, u * C:(u + 1) * C, :] = o + jnp.dot(sc, v, preferred_element_type=F32)

    def main(gi, finalize_previous):
        lax.cond(safe_ref[gi] == 1, lambda: group(gi, True, finalize_previous),
                 lambda: group(gi, False, finalize_previous))

    def body(gi, _):
        main(gi, True)
        return 0

    st_ref[...] = jnp.zeros_like(st_ref)
    main(0, False)
    lax.fori_loop(1, n_groups, body, 0)
    finalize(n_groups - 1)


def _gla(gqk, gv, gout, bcum, gnorm):
    b, s, _ = gv.shape
    n_groups, rem = divmod(s, GLA_CHUNK * GLA_GROUP)
    assert rem == 0, s
    return pl.pallas_call(
        _gla_kernel,
        grid=(b, GLA_HEADS),
        in_specs=[
            pl.BlockSpec((1, s, GLA_DK), lambda bi, h: (bi, 0, h)),
            pl.BlockSpec((1, s, GLA_DK), lambda bi, h: (bi, 0, GLA_HEADS + h)),
            pl.BlockSpec((1, s, GLA_DV), lambda bi, h: (bi, 0, h)),
            pl.BlockSpec((1, s, GLA_DV), lambda bi, h: (bi, 0, h)),
            pl.BlockSpec((1, s, GLA_DK), lambda bi, h: (bi, 0, h)),
            pl.BlockSpec((1, GLA_DV), lambda bi, h: (0, 0)),
        ],
        out_specs=pl.BlockSpec((1, s, GLA_DV), lambda bi, h: (bi, 0, h)),
        out_shape=jax.ShapeDtypeStruct((b, s, GLA_VALUE_WIDTH), BF16),
        scratch_shapes=[pltpu.VMEM((GLA_DV, GLA_DK), F32),
                        pltpu.VMEM((2, GLA_CHUNK * GLA_GROUP, GLA_DV), F32),
                        pltpu.SMEM((n_groups,), jnp.int32)],
        compiler_params=pltpu.CompilerParams(
            dimension_semantics=("parallel", "parallel"), vmem_limit_bytes=VMEM_LIMIT),
        name="gla",
    )(gqk, gqk, gv, gout, bcum, gnorm)


def _tail_kernel(x_ref, osb_ref, ogla_ref, gates_ref, wpa_ref, wpb_ref, wout_ref,
                 g_ref, w1_ref, w2_ref, gf_ref, o_ref, *, ff_chunk):
    d = x_ref.shape[-1]
    a = jnp.dot(osb_ref[0], wpa_ref[...], preferred_element_type=F32)
    c = jnp.dot(ogla_ref[0], wpb_ref[...], preferred_element_type=F32)
    ga = jax.nn.sigmoid(gates_ref[0, :, 0:d].astype(F32))
    gb = jax.nn.sigmoid(gates_ref[0, :, d:2 * d].astype(F32))
    mix = ga * a + gb * c
    x = x_ref[0] + jnp.dot(mix.astype(BF16), wout_ref[...], preferred_element_type=F32)

    h = _rms(x, g_ref[...]).astype(BF16)
    acc = jnp.zeros_like(x)
    for j in range(w1_ref.shape[1] // ff_chunk):
        u = jnp.dot(h, w1_ref[:, j * ff_chunk:(j + 1) * ff_chunk], preferred_element_type=F32)
        u = jnp.square(jnp.maximum(u, 0.0)).astype(BF16)
        acc = acc + jnp.dot(u, w2_ref[j * ff_chunk:(j + 1) * ff_chunk, :], preferred_element_type=F32)
    o_ref[0] = _rms(x + acc, gf_ref[...])


def _tail(x, osb, ogla, gates, wpa, wpb, wout, g, w1, w2, gf, tm, ff_chunk):
    b, s, d = x.shape
    tok = lambda width: pl.BlockSpec((1, tm, width), lambda bi, i: (bi, i, 0))
    vec = pl.BlockSpec((1, d), lambda bi, i: (0, 0))
    full = lambda w: pl.BlockSpec(w.shape, lambda bi, i: (0, 0), pipeline_mode=pl.Buffered(1))
    return pl.pallas_call(
        functools.partial(_tail_kernel, ff_chunk=ff_chunk),
        grid=(b, s // tm),
        in_specs=[tok(d), tok(SB_WIDTH), tok(GLA_VALUE_WIDTH), tok(2 * d), full(wpa), full(wpb),
                  full(wout), vec, full(w1), full(w2), vec],
        out_specs=tok(d),
        out_shape=jax.ShapeDtypeStruct((b, s, d), F32),
        compiler_params=pltpu.CompilerParams(
            dimension_semantics=("parallel", "parallel"), vmem_limit_bytes=VMEM_LIMIT),
        name="merge_mlp",
    )(x, osb, ogla, gates, wpa, wpb, wout, g, w1, w2, gf)


def _layer(x, norm_mix, w_in, w_gate_up, b_gate_up, gla_norm, w_proj_sb, w_proj_gla, w_out,
           norm_mlp, w_ff1, w_ff2, final_gain, tm):
    d = x.shape[-1]
    c_low = 3 * SB_WIDTH + 2 * GLA_KEY_WIDTH + 2 * GLA_VALUE_WIDTH
    w_main = w_in[:, :c_low].astype(BF16)
    w_gates = w_in[:, c_low + GLA_GATE_RANK:].astype(BF16)
    w_low = jnp.pad(w_in[:, c_low:c_low + GLA_GATE_RANK].astype(BF16),
                    ((0, 0), (0, LANES - GLA_GATE_RANK)))
    wup = jnp.concatenate(
        [w_gate_up, jnp.zeros((LANES - GLA_GATE_RANK, GLA_KEY_WIDTH), w_gate_up.dtype)], axis=0).astype(BF16)

    sbq, sbk, sbv, gqk, gv, gout, gates, bcum = _inproj(
        x, norm_mix.reshape(1, d), w_main, w_gates, w_low, wup, b_gate_up.reshape(1, -1), tm)
    o_sb = _sb_attention(sbq, sbk, sbv)
    o_gla = _gla(gqk, gv, gout, bcum, gla_norm.reshape(1, -1))
    return _tail(x, o_sb, o_gla, gates, w_proj_sb.astype(BF16), w_proj_gla.astype(BF16),
                 w_out.astype(BF16), norm_mlp.reshape(1, d), w_ff1.astype(BF16), w_ff2.astype(BF16),
                 final_gain, tm, ff_chunk=1024)


def kernel(x, norm_mix, w_in, w_gate_up, b_gate_up, gla_norm, w_proj_sb, w_proj_gla, w_out,
           norm_mlp, w_ff1, w_ff2, norm_final):
    depth = w_in.shape[0]
    s = x.shape[1]
    tm = min(512, s)
    assert depth == 1, "the final RMSNorm is fused into the last layer's MLP kernel"
    return _layer(x, norm_mix[0], w_in[0], w_gate_up[0], b_gate_up[0], gla_norm[0], w_proj_sb[0],
                  w_proj_gla[0], w_out[0], norm_mlp[0], w_ff1[0], w_ff2[0],
                  norm_final.reshape(1, -1), tm)
```

```python
import functools

import jax
import jax.numpy as jnp
from jax import lax
from jax.experimental import pallas as pl
from jax.experimental.pallas import tpu as pltpu

F32 = jnp.float32
BF16 = jnp.bfloat16
EPS = 1e-6

SB_HEADS = 8
SB_DIM = 64
SB_WIDTH = SB_HEADS * SB_DIM
GLA_HEADS = 4
GLA_DK = 128
GLA_DV = 256
GLA_KEY_WIDTH = GLA_HEADS * GLA_DK
GLA_VALUE_WIDTH = GLA_HEADS * GLA_DV
GLA_GATE_RANK = 16
GLA_GATE_TAU = 16.0
GLA_CHUNK = 128
GLA_SUB = 16
GLA_GROUP = 4
GLA_SAFE_SPREAD = 60.0
LANES = 128

SB_TQ = 128
SB_NEAR = 2
SB_FAR = 64
SB_QB = 4
SB_UNDERFLOW = -110.0

VMEM_LIMIT = 56 * 1024 * 1024

NT_DIMS = (((1,), (1,)), ((), ()))
TN_DIMS = (((0,), (0,)), ((), ()))


def _inv_rms(x):
    return lax.rsqrt(jnp.mean(x * x, axis=-1, keepdims=True) + EPS)


def _rms(x, g):
    return (x * _inv_rms(x)) * g


def _log_sigmoid(x):
    return -(jnp.maximum(-x, 0.0) + jnp.log(1.0 + jnp.exp(-jnp.abs(x))))


def _inproj_kernel(x_ref, g_ref, wa_ref, wg_ref, wl_ref, wup_ref, bup_ref, sbq_ref, sbk_ref, sbv_ref,
                   gqk_ref, gv_ref, gout_ref, gates_ref, bc_ref):
    x = x_ref[0]
    inv_rms = _inv_rms(x)
    xg = (x * g_ref[...]).astype(BF16)

    def proj(w_ref, c0, width):
        w = w_ref[:, c0:c0 + width].astype(BF16)
        return jnp.dot(xg, w, preferred_element_type=F32) * inv_rms

    def sb_proj(idx, ref):
        r = proj(wa_ref, idx * SB_WIDTH, SB_WIDTH)
        if idx == 0:
            r = r * (SB_DIM ** -0.5)
        ref[0] = r.astype(BF16)

    low = proj(wl_ref, 0, LANES).astype(BF16)
    sb_proj(0, sbq_ref)
    gate = jnp.dot(low, wup_ref[...], preferred_element_type=F32) + bup_ref[...]
    sb_proj(1, sbk_ref)
    log_a = _log_sigmoid(gate) * (1.0 / GLA_GATE_TAU)
    hi = log_a.astype(BF16)
    r1 = log_a - hi.astype(F32)
    mid = r1.astype(BF16)
    lo = (r1 - mid.astype(F32)).astype(BF16)
    sb_proj(2, sbv_ref)
    c = 3 * SB_WIDTH
    gqk_ref[0] = proj(wa_ref, c, 2 * GLA_KEY_WIDTH).astype(BF16)
    c += 2 * GLA_KEY_WIDTH
    C = GLA_CHUNK
    row = lax.broadcasted_iota(jnp.int32, (C, 3 * C), 0)
    col = lax.broadcasted_iota(jnp.int32, (C, 3 * C), 1)
    ltri3 = ((col % C) <= row).astype(BF16)
    for c0 in range(0, log_a.shape[0], C):
        parts = jnp.concatenate([p[c0:c0 + C] for p in (hi, mid, lo)], axis=0)
        bc_ref[0, c0:c0 + C, :] = jnp.dot(ltri3, parts, preferred_element_type=F32)
    for ref in (gv_ref, gout_ref):
        ref[0] = proj(wa_ref, c, GLA_VALUE_WIDTH).astype(BF16)
        c += GLA_VALUE_WIDTH
    d_model = x_ref.shape[-1]
    for half in range(2):
        gates_ref[0, :, half * d_model:(half + 1) * d_model] = (
            proj(wg_ref, half * d_model, d_model).astype(BF16))


def _inproj(x, norm_g, w_in, w_main_cols, w_gates, w_low, wup, bup, tm):
    b, s, d = x.shape
    assert tm % GLA_CHUNK == 0
    grid = (b, s // tm)
    tok = lambda width: pl.BlockSpec((1, tm, width), lambda bi, i: (bi, i, 0))
    out_shape = (
        jax.ShapeDtypeStruct((b, s, SB_WIDTH), BF16),
        jax.ShapeDtypeStruct((b, s, SB_WIDTH), BF16),
        jax.ShapeDtypeStruct((b, s, SB_WIDTH), BF16),
        jax.ShapeDtypeStruct((b, s, 2 * GLA_KEY_WIDTH), BF16),
        jax.ShapeDtypeStruct((b, s, GLA_VALUE_WIDTH), BF16),
        jax.ShapeDtypeStruct((b, s, GLA_VALUE_WIDTH), BF16),
        jax.ShapeDtypeStruct((b, s, 2 * d), BF16),
        jax.ShapeDtypeStruct((b, s, GLA_KEY_WIDTH), F32),
    )
    return pl.pallas_call(
        _inproj_kernel,
        grid=grid,
        in_specs=[
            tok(d),
            pl.BlockSpec((1, d), lambda bi, i: (0, 0)),
            pl.BlockSpec((d, w_main_cols), lambda bi, i: (0, 0), pipeline_mode=pl.Buffered(1)),
        ] + [pl.BlockSpec(w.shape, lambda bi, i: (0, 0), pipeline_mode=pl.Buffered(1))
             for w in (w_gates, w_low, wup, bup)],
        out_specs=(tok(SB_WIDTH), tok(SB_WIDTH), tok(SB_WIDTH), tok(2 * GLA_KEY_WIDTH), tok(GLA_VALUE_WIDTH),
                   tok(GLA_VALUE_WIDTH), tok(2 * d), tok(GLA_KEY_WIDTH)),
        out_shape=out_shape,
        compiler_params=pltpu.CompilerParams(
            dimension_semantics=("parallel", "parallel"), vmem_limit_bytes=VMEM_LIMIT),
        name="inproj",
    )(x, norm_g, w_in, w_gates, w_low, wup, bup)


def _aligned(start, tile):
    return start if isinstance(start, int) else pl.multiple_of(start, tile)


def _sb_tile(z, neg_tri2, mask):
    tk = z.shape[1]
    f = jnp.maximum(z, 0.0) + jnp.log(1.0 + jnp.exp(-jnp.abs(z)))
    if mask is not None:
        f = jnp.where(mask, f, 0.0)
    hi = f.astype(BF16)
    lo = (f - hi.astype(F32)).astype(BF16)
    cs = jnp.dot(jnp.concatenate([hi, lo], axis=1), neg_tri2, preferred_element_type=F32)
    return z + cs[:, :tk], cs[:, tk:]


def _sb_kernel(q_ref, k_ref, v_ref, o_ref, acc_ref, carry_ref):
    s_len = q_ref.shape[1]
    heads = range(q_ref.shape[2] // SB_DIM)
    T = SB_TQ
    nq = s_len // T
    lane_head = lax.broadcasted_iota(jnp.int32, (T, q_ref.shape[2]), 1) // SB_DIM
    row = lax.broadcasted_iota(jnp.int32, (T, T), 0)
    col = lax.broadcasted_iota(jnp.int32, (T, T), 1)
    neg_tri = jnp.where(row >= col, -1.0, 0.0).astype(BF16)
    half = jnp.concatenate([neg_tri, jnp.full((T, T), -1.0, BF16)], axis=1)
    neg_tri2 = jnp.concatenate([half, half], axis=0)
    past = col < row

    r2 = lax.broadcasted_iota(jnp.int32, (2 * T, 2 * T), 0) % T
    c2 = lax.broadcasted_iota(jnp.int32, (2 * T, 2 * T), 1)
    same_head = r2 // SB_FAR == (c2 % T) // SB_FAR
    far_tri2 = jnp.where(
        jnp.logical_and(same_head, jnp.logical_or(c2 >= T, r2 >= c2)), -1.0, 0.0).astype(BF16)
    lane_head_far = lax.broadcasted_iota(jnp.int32, (SB_FAR, q_ref.shape[2]), 1) // SB_DIM

    def windows(blocks, n_near, with_far):
        chains = [(u, hh) for u in range(len(blocks)) for hh in heads]
        q0s = [_aligned(i * T, T) for i in blocks]
        k0s = [_aligned((i - (n_near - 1)) * T, T) for i in blocks]
        f0s = [_aligned((i - (n_near - 1)) * T - SB_FAR, SB_FAR) for i in blocks]
        zs = []
        for u, hh in chains:
            q = head_only(q_ref[0, pl.ds(q0s[u], T), :], hh)
            zs.append(lax.dot_general(q, k_ref[0, pl.ds(k0s[u], n_near * T), :], NT_DIMS,
                                      preferred_element_type=F32))
        z_fars = []
        if with_far:
            for u in range(len(blocks)):
                k_far = k_ref[0, pl.ds(f0s[u], SB_FAR), :]
                keys = jnp.concatenate(
                    [jnp.where(lane_head_far == hh, k_far, jnp.zeros_like(k_far)) for hh in heads], axis=0)
                z_fars.append(lax.dot_general(q_ref[0, pl.ds(q0s[u], T), :], keys, NT_DIMS,
                                              preferred_element_type=F32))
        stats = []
        for z in zs:
            per_tile = []
            for jb in reversed(range(n_near)):
                mask = past if jb == n_near - 1 else None
                per_tile.append(_sb_tile(z[:, jb * T:(jb + 1) * T], neg_tri2, mask))
            stats.append(per_tile)
        far_stats = [_sb_tile(z, far_tri2, None) for z in z_fars]
        near = [([], []) for _ in blocks]
        for (u, hh), per_tile in zip(chains, stats):
            carry = jnp.zeros((T, T), F32)
            ws = []
            for n, (logw, total) in enumerate(per_tile):
                w = jnp.exp(logw + carry)
                if n == 0:
                    w = jnp.where(past, w, 0.0)
                ws.append(w.astype(BF16))
                carry = carry + total
            w_all = ws[0] if n_near == 1 else jnp.concatenate(ws[::-1], axis=1)
            near[u][0].append(jnp.dot(w_all, v_ref[0, pl.ds(k0s[u], n_near * T), :],
                                      preferred_element_type=F32))
            near[u][1].append(carry)
        out = []
        for u, (accs, carries) in enumerate(near):
            slab, carry = by_head(accs), by_head(carries)
            if with_far:
                logw, total = far_stats[u]
                v_far = v_ref[0, pl.ds(f0s[u], SB_FAR), :]
                values = jnp.concatenate(
                    [jnp.where(lane_head_far == hh, v_far, jnp.zeros_like(v_far)) for hh in heads], axis=0)
                slab = slab + jnp.dot(jnp.exp(logw + carry).astype(BF16), values,
                                      preferred_element_type=F32)
                carry = carry + total
            out.append((accs, carries, slab, carry))
        return out

    def largest(cs):
        m = cs[0]
        for c in cs[1:]:
            m = jnp.maximum(m, c)
        return jnp.max(m)

    def head_only(a, hh):
        return jnp.where(lane_head == hh, a, jnp.zeros_like(a))

    def by_head(per_head):
        out = per_head[0]
        for hh in list(heads)[1:]:
            out = jnp.where(lane_head == hh, per_head[hh], out)
        return out

    def store(i, slab):
        o_ref[0, pl.ds(_aligned(i * T, T), T), :] = slab.astype(BF16)

    def finish(i, jb_first, accs, carries, cmax):
        q0 = _aligned(i * T, T)
        for n in range(len(accs)):
            acc_ref[n] = accs[n]
            carry_ref[n] = carries[n]

        def cond(st):
            jb, cmax = st
            return jnp.logical_and(jb >= 0, cmax > SB_UNDERFLOW)

        def body(st):
            jb, _ = st
            k0 = pl.multiple_of(jb * T, T)
            for n, hh in enumerate(heads):
                q = head_only(q_ref[0, pl.ds(q0, T), :], hh)
                z = lax.dot_general(q, k_ref[0, pl.ds(k0, T), :], NT_DIMS,
                                    preferred_element_type=F32)
                logw, total = _sb_tile(z, neg_tri2, None)
                w = jnp.exp(logw + carry_ref[n]).astype(BF16)
                acc_ref[n] += jnp.dot(w, v_ref[0, pl.ds(k0, T), :],
                                      preferred_element_type=F32)
                carry_ref[n] += total
            return jb - 1, largest([carry_ref[n] for n in range(len(accs))])

        lax.while_loop(cond, body, (jb_first, cmax))
        store(i, by_head([acc_ref[n] for n in range(len(accs))]))

    def run(blocks, n_near, with_far=False):
        results = windows(blocks, n_near, with_far)
        for i, (_, _, slab, _) in zip(blocks, results):
            store(i, slab)
        firsts = [i - n_near for i in blocks]
        if all(isinstance(jb, int) and jb < 0 for jb in firsts):
            return
        cmaxs = [jnp.max(carry) for _, _, _, carry in results]
        unfinished = [jnp.logical_and(jb >= 0, cmax > SB_UNDERFLOW) for jb, cmax in zip(firsts, cmaxs)]

        @pl.when(functools.reduce(jnp.logical_or, unfinished))
        def _():
            for i, jb, (accs, carries, _, _), cmax, more in zip(blocks, firsts, results, cmaxs, unfinished):
                @pl.when(more)
                def _(i=i, jb=jb, accs=accs, carries=carries, cmax=cmax):
                    finish(i, jnp.int32(jb), accs, carries, cmax)

    n_near = min(SB_NEAR, nq)
    for i in range(n_near - 1):
        run([i], i + 1)
    if nq > n_near - 1:
        run([n_near - 1], n_near)
    n_lead = n_near + (nq - n_near) % SB_QB
    if n_lead > n_near:
        run(list(range(n_near, n_lead)), n_near, True)

    def q_body(p, _):
        run([n_lead + p * SB_QB + u for u in range(SB_QB)], n_near, True)
        return 0

    lax.fori_loop(0, (nq - n_lead) // SB_QB, q_body, 0)


def _sb_attention(q, k, v):
    b, s, _ = q.shape
    hp = LANES // SB_DIM
    spec = pl.BlockSpec((1, s, LANES), lambda bi, h: (bi, 0, h))
    return pl.pallas_call(
        _sb_kernel,
        grid=(b, SB_HEADS // hp),
        in_specs=[spec, spec, spec],
        out_specs=spec,
        out_shape=jax.ShapeDtypeStruct((b, s, SB_WIDTH), BF16),
        scratch_shapes=[pltpu.VMEM((hp, SB_TQ, LANES), F32),
                        pltpu.VMEM((hp, SB_TQ, SB_TQ), F32)],
        compiler_params=pltpu.CompilerParams(
            dimension_semantics=("parallel", "parallel"), vmem_limit_bytes=VMEM_LIMIT),
        name="sb_attention",
    )(q, k, v)


def _rows_bcast(a, offset):
    n_sub = a.shape[0] // GLA_SUB
    return jnp.concatenate(
        [jnp.broadcast_to(a[i * GLA_SUB + offset:i * GLA_SUB + offset + 1, :], (GLA_SUB, a.shape[1]))
         for i in range(n_sub)], axis=0)


def _gla_kernel(q_ref, k_ref, v_ref, go_ref, bc_ref, gn_ref, o_ref, st_ref, raw_ref, safe_ref):
    s_len = q_ref.shape[1]
    C = GLA_CHUNK
    G = GLA_GROUP
    n_sub = C // GLA_SUB
    scale = GLA_DK ** -0.5

    r_cc = lax.broadcasted_iota(jnp.int32, (C, C), 0)
    c_cc = lax.broadcasted_iota(jnp.int32, (C, C), 1)
    causal = c_cc <= r_cc
    diag_keep = jnp.logical_and((r_cc // GLA_SUB) == (c_cc // GLA_SUB), causal)
    r_loc = lax.broadcasted_iota(jnp.int32, (C, GLA_DK), 0) % GLA_SUB
    row_pick = [(r_loc == s).astype(BF16) for s in range(GLA_SUB)]

    n_groups = s_len // (C * G)
    n_chunks = s_len // C
    spread = (bc_ref[0, pl.ds(0, n_chunks, stride=C), :]
              - bc_ref[0, pl.ds(C - 1, n_chunks, stride=C), :])
    for g in range(n_groups):
        safe_ref[g] = (jnp.max(spread[g * G:(g + 1) * G]) < GLA_SAFE_SPREAD).astype(jnp.int32)

    def scores_factored(qs, ks, bcums):
        out = []
        for q, k, bcum in zip(qs, ks, bcums):
            r0 = bcum[0:1, :]
            qn = (q * jnp.exp(bcum - r0)).astype(BF16)
            kn = (k * jnp.exp(r0 - bcum)).astype(BF16)
            sc = lax.dot_general(qn, kn, NT_DIMS, preferred_element_type=F32)
            out.append(jnp.where(causal, sc, 0.0).astype(BF16))
        return out

    def scores_bounded(qs, ks, t0s, bcums):
        offs = []
        for q, k, bcum in zip(qs, ks, bcums):
            qn = (q * jnp.exp(bcum - _rows_bcast(bcum, 0))).astype(BF16)
            off = [jnp.zeros((GLA_SUB, C), F32)]
            for i in range(1, n_sub):
                n_prev = i * GLA_SUB
                ref_row = bcum[n_prev:n_prev + 1, :]
                kn = jnp.concatenate(
                    [(k[:n_prev] * jnp.exp(ref_row - bcum[:n_prev])).astype(BF16),
                     jnp.zeros((C - n_prev, GLA_DK), BF16)], axis=0)
                off.append(lax.dot_general(qn[i * GLA_SUB:(i + 1) * GLA_SUB], kn, NT_DIMS,
                                           preferred_element_type=F32))
            offs.append(jnp.concatenate(off, axis=0))
        out = []
        for q, t0, bcum, off in zip(qs, t0s, bcums, offs):
            kb = k_ref[0, pl.ds(t0, C), :]
            prods, keys = [], []
            for s in range(GLA_SUB):
                decay = jnp.exp(jnp.minimum(bcum - _rows_bcast(bcum, s), 0.0))
                prods.append((q * decay).astype(BF16))
                keys.append(kb * row_pick[s])
            diag = lax.dot_general(jnp.concatenate(prods, axis=1), jnp.concatenate(keys, axis=1),
                                   NT_DIMS, preferred_element_type=F32)
            out.append(jnp.where(diag_keep, diag, off).astype(BF16))
        return out

    def finalize(gi):
        slot = gi % 2
        for u in range(G):
            t0 = pl.multiple_of((gi * G + u) * C, C)
            y = _rms(raw_ref[slot, u * C:(u + 1) * C, :], gn_ref[...])
            g = go_ref[0, pl.ds(t0, C), :].astype(F32)
            o_ref[0, pl.ds(t0, C), :] = (y * (g * jax.nn.sigmoid(g))).astype(BF16)

    def group(gi, factored, finalize_previous):
        if finalize_previous:
            finalize(gi - 1)
        t0s = [pl.multiple_of((gi * G + u) * C, C) for u in range(G)]
        qs = [q_ref[0, pl.ds(t0, C), :].astype(F32) * scale for t0 in t0s]
        ks = [k_ref[0, pl.ds(t0, C), :].astype(F32) for t0 in t0s]
        vs = [v_ref[0, pl.ds(t0, C), :] for t0 in t0s]
        bcums = [bc_ref[0, pl.ds(t0, C), :] for t0 in t0s]
        scores = scores_factored(qs, ks, bcums) if factored else scores_bounded(qs, ks, t0s, bcums)

        kvs, q_ins, decays = [], [], []
        for q, k, v, bcum in zip(qs, ks, vs, bcums):
            b_last = bcum[C - 1:C, :]
            k_dec = (k * jnp.exp(b_last - bcum)).astype(BF16)
            kvs.append(lax.dot_general(v, k_dec, TN_DIMS, preferred_element_type=F32))
            q_ins.append((q * jnp.exp(bcum)).astype(BF16))
            decays.append(jnp.exp(b_last))
        st = st_ref[...]
        outs = []
        for q_in, kv, decay in zip(q_ins, kvs, decays):
            outs.append(lax.dot_general(q_in, st.astype(BF16), NT_DIMS, preferred_element_type=F32))
            st = st * decay + kv
        st_ref[...] = st

        slot = gi % 2
        for u, (o, sc, v) in enumerate(zip(outs, scores, vs)):
            raw_ref[slot, u * C:(u + 1) * C, :] = o + jnp.dot(sc, v, preferred_element_type=F32)

    def main(gi, finalize_previous):
        lax.cond(safe_ref[gi] == 1, lambda: group(gi, True, finalize_previous),
                 lambda: group(gi, False, finalize_previous))

    def body(gi, _):
        main(gi, True)
        return 0

    st_ref[...] = jnp.zeros_like(st_ref)
    main(0, False)
    lax.fori_loop(1, n_groups, body, 0)
    finalize(n_groups - 1)


def _gla(gqk, gv, gout, bcum, gnorm):
    b, s, _ = gv.shape
    n_groups, rem = divmod(s, GLA_CHUNK * GLA_GROUP)
    assert rem == 0, s
    return pl.pallas_call(
        _gla_kernel,
        grid=(b, GLA_HEADS),
        in_specs=[
            pl.BlockSpec((1, s, GLA_DK), lambda bi, h: (bi, 0, h)),
            pl.BlockSpec((1, s, GLA_DK), lambda bi, h: (bi, 0, GLA_HEADS + h)),
            pl.BlockSpec((1, s, GLA_DV), lambda bi, h: (bi, 0, h)),
            pl.BlockSpec((1, s, GLA_DV), lambda bi, h: (bi, 0, h)),
            pl.BlockSpec((1, s, GLA_DK), lambda bi, h: (bi, 0, h)),
            pl.BlockSpec((1, GLA_DV), lambda bi, h: (0, 0)),
        ],
        out_specs=pl.BlockSpec((1, s, GLA_DV), lambda bi, h: (bi, 0, h)),
        out_shape=jax.ShapeDtypeStruct((b, s, GLA_VALUE_WIDTH), BF16),
        scratch_shapes=[pltpu.VMEM((GLA_DV, GLA_DK), F32),
                        pltpu.VMEM((2, GLA_CHUNK * GLA_GROUP, GLA_DV), F32),
                        pltpu.SMEM((n_groups,), jnp.int32)],
        compiler_params=pltpu.CompilerParams(
            dimension_semantics=("parallel", "parallel"), vmem_limit_bytes=VMEM_LIMIT),
        name="gla",
    )(gqk, gqk, gv, gout, bcum, gnorm)


def _tail_kernel(x_ref, osb_ref, ogla_ref, gates_ref, wpa_ref, wpb_ref, wout_ref,
                 g_ref, w1_ref, w2_ref, gf_ref, o_ref, *, ff_chunk):
    d = x_ref.shape[-1]
    a = jnp.dot(osb_ref[0], wpa_ref[...], preferred_element_type=F32)
    c = jnp.dot(ogla_ref[0], wpb_ref[...], preferred_element_type=F32)
    ga = jax.nn.sigmoid(gates_ref[0, :, 0:d].astype(F32))
    gb = jax.nn.sigmoid(gates_ref[0, :, d:2 * d].astype(F32))
    mix = ga * a + gb * c
    x = x_ref[0] + jnp.dot(mix.astype(BF16), wout_ref[...], preferred_element_type=F32)

    inv_rms = _inv_rms(x)
    h = (x * g_ref[...]).astype(BF16)
    acc = jnp.zeros_like(x)
    for j in range(w1_ref.shape[1] // ff_chunk):
        u = jnp.dot(h, w1_ref[:, j * ff_chunk:(j + 1) * ff_chunk], preferred_element_type=F32)
        u = jnp.square(jnp.maximum(u, 0.0) * inv_rms).astype(BF16)
        acc = acc + jnp.dot(u, w2_ref[j * ff_chunk:(j + 1) * ff_chunk, :], preferred_element_type=F32)
    o_ref[0] = _rms(x + acc, gf_ref[...])


def _tail(x, osb, ogla, gates, wpa, wpb, wout, g, w1, w2, gf, tm, ff_chunk):
    b, s, d = x.shape
    tok = lambda width: pl.BlockSpec((1, tm, width), lambda bi, i: (bi, i, 0))
    vec = pl.BlockSpec((1, d), lambda bi, i: (0, 0))
    full = lambda w: pl.BlockSpec(w.shape, lambda bi, i: (0, 0), pipeline_mode=pl.Buffered(1))
    return pl.pallas_call(
        functools.partial(_tail_kernel, ff_chunk=ff_chunk),
        grid=(b, s // tm),
        in_specs=[tok(d), tok(SB_WIDTH), tok(GLA_VALUE_WIDTH), tok(2 * d), full(wpa), full(wpb),
                  full(wout), vec, full(w1), full(w2), vec],
        out_specs=tok(d),
        out_shape=jax.ShapeDtypeStruct((b, s, d), F32),
        compiler_params=pltpu.CompilerParams(
            dimension_semantics=("parallel", "parallel"), vmem_limit_bytes=VMEM_LIMIT),
        name="merge_mlp",
    )(x, osb, ogla, gates, wpa, wpb, wout, g, w1, w2, gf)


def _layer(x, norm_mix, w_in, w_gate_up, b_gate_up, gla_norm, w_proj_sb, w_proj_gla, w_out,
           norm_mlp, w_ff1, w_ff2, final_gain, tm):
    d = x.shape[-1]
    c_low = 3 * SB_WIDTH + 2 * GLA_KEY_WIDTH + 2 * GLA_VALUE_WIDTH
    w_gates = w_in[:, c_low + GLA_GATE_RANK:]
    w_low = jnp.pad(w_in[:, c_low:c_low + GLA_GATE_RANK], ((0, 0), (0, LANES - GLA_GATE_RANK)))
    wup = jnp.concatenate(
        [w_gate_up, jnp.zeros((LANES - GLA_GATE_RANK, GLA_KEY_WIDTH), w_gate_up.dtype)], axis=0).astype(BF16)

    sbq, sbk, sbv, gqk, gv, gout, gates, bcum = _inproj(
        x, norm_mix.reshape(1, d), w_in, c_low, w_gates, w_low, wup, b_gate_up.reshape(1, -1), tm)
    o_sb = _sb_attention(sbq, sbk, sbv)
    o_gla = _gla(gqk, gv, gout, bcum, gla_norm.reshape(1, -1))
    return _tail(x, o_sb, o_gla, gates, w_proj_sb.astype(BF16), w_proj_gla.astype(BF16),
                 w_out.astype(BF16), norm_mlp.reshape(1, d), w_ff1.astype(BF16), w_ff2.astype(BF16),
                 final_gain, tm, ff_chunk=1024)


def kernel(x, norm_mix, w_in, w_gate_up, b_gate_up, gla_norm, w_proj_sb, w_proj_gla, w_out,
           norm_mlp, w_ff1, w_ff2, norm_final):
    depth = w_in.shape[0]
    s = x.shape[1]
    tm = min(512, s)
    assert depth == 1, "the final RMSNorm is fused into the last layer's MLP kernel"
    return _layer(x, norm_mix[0], w_in[0], w_gate_up[0], b_gate_up[0], gla_norm[0], w_proj_sb[0],
                  w_proj_gla[0], w_out[0], norm_mlp[0], w_ff1[0], w_ff2[0],
                  norm_final.reshape(1, -1), tm)
```

```python
import functools

import jax
import jax.numpy as jnp
from jax import lax
from jax.experimental import pallas as pl
from jax.experimental.pallas import tpu as pltpu

F32 = jnp.float32
BF16 = jnp.bfloat16
EPS = 1e-6

SB_HEADS = 8
SB_DIM = 64
SB_WIDTH = SB_HEADS * SB_DIM
GLA_HEADS = 4
GLA_DK = 128
GLA_DV = 256
GLA_KEY_WIDTH = GLA_HEADS * GLA_DK
GLA_VALUE_WIDTH = GLA_HEADS * GLA_DV
GLA_GATE_RANK = 16
GLA_GATE_TAU = 16.0
GLA_CHUNK = 128
GLA_SUB = 16
GLA_GROUP = 4
GLA_SAFE_SPREAD = 60.0
LANES = 128

SB_TQ = 128
SB_NEAR = 2
SB_FAR = 64
SB_QB = 4
SB_UNDERFLOW = -110.0

VMEM_LIMIT = 56 * 1024 * 1024

NT_DIMS = (((1,), (1,)), ((), ()))
TN_DIMS = (((0,), (0,)), ((), ()))


def _inv_rms(x):
    return lax.rsqrt(jnp.mean(x * x, axis=-1, keepdims=True) + EPS)


def _rms(x, g):
    return (x * _inv_rms(x)) * g


def _log_sigmoid(x):
    return -(jnp.maximum(-x, 0.0) + jnp.log(1.0 + jnp.exp(-jnp.abs(x))))


def _inproj_kernel(x_ref, g_ref, wa_ref, wt_ref, wup_ref, bup_ref, sbq_ref, sbk_ref, sbv_ref,
                   gqk_ref, gv_ref, gout_ref, gates_ref, bc_ref, wg_ref):
    d_model = x_ref.shape[-1]

    @pl.when(jnp.logical_and(pl.program_id(0) == 0, pl.program_id(1) == 0))
    def _():
        wg_ref[...] = wt_ref[:, GLA_GATE_RANK:GLA_GATE_RANK + 2 * d_model].astype(BF16)

    x = x_ref[0]
    inv_rms = _inv_rms(x)
    xg = (x * g_ref[...]).astype(BF16)

    def proj(w_ref, c0, width):
        w = w_ref[:, c0:c0 + width].astype(BF16)
        return jnp.dot(xg, w, preferred_element_type=F32) * inv_rms

    def sb_proj(idx, ref):
        r = proj(wa_ref, idx * SB_WIDTH, SB_WIDTH)
        if idx == 0:
            r = r * (SB_DIM ** -0.5)
        ref[0] = r.astype(BF16)

    low = proj(wt_ref, 0, LANES).astype(BF16)
    sb_proj(0, sbq_ref)
    gate = jnp.dot(low, wup_ref[...], preferred_element_type=F32) + bup_ref[...]
    sb_proj(1, sbk_ref)
    log_a = _log_sigmoid(gate) * (1.0 / GLA_GATE_TAU)
    hi = log_a.astype(BF16)
    r1 = log_a - hi.astype(F32)
    mid = r1.astype(BF16)
    lo = (r1 - mid.astype(F32)).astype(BF16)
    sb_proj(2, sbv_ref)
    c = 3 * SB_WIDTH
    gqk_ref[0] = proj(wa_ref, c, 2 * GLA_KEY_WIDTH).astype(BF16)
    c += 2 * GLA_KEY_WIDTH
    C = GLA_CHUNK
    row = lax.broadcasted_iota(jnp.int32, (C, 3 * C), 0)
    col = lax.broadcasted_iota(jnp.int32, (C, 3 * C), 1)
    ltri3 = ((col % C) <= row).astype(BF16)
    for c0 in range(0, log_a.shape[0], C):
        parts = jnp.concatenate([p[c0:c0 + C] for p in (hi, mid, lo)], axis=0)
        bc_ref[0, c0:c0 + C, :] = jnp.dot(ltri3, parts, preferred_element_type=F32)
    for ref in (gv_ref, gout_ref):
        ref[0] = proj(wa_ref, c, GLA_VALUE_WIDTH).astype(BF16)
        c += GLA_VALUE_WIDTH
    for half in range(2):
        gates_ref[0, :, half * d_model:(half + 1) * d_model] = (
            proj(wg_ref, half * d_model, d_model).astype(BF16))


def _inproj(x, norm_g, w_in, w_main_cols, wup, bup, tm):
    b, s, d = x.shape
    tail_cols = -(-(w_in.shape[1] - w_main_cols) // LANES) * LANES
    while w_main_cols % tail_cols:
        tail_cols += LANES
    assert tm % GLA_CHUNK == 0
    grid = (b, s // tm)
    tok = lambda width: pl.BlockSpec((1, tm, width), lambda bi, i: (bi, i, 0))
    out_shape = (
        jax.ShapeDtypeStruct((b, s, SB_WIDTH), BF16),
        jax.ShapeDtypeStruct((b, s, SB_WIDTH), BF16),
        jax.ShapeDtypeStruct((b, s, SB_WIDTH), BF16),
        jax.ShapeDtypeStruct((b, s, 2 * GLA_KEY_WIDTH), BF16),
        jax.ShapeDtypeStruct((b, s, GLA_VALUE_WIDTH), BF16),
        jax.ShapeDtypeStruct((b, s, GLA_VALUE_WIDTH), BF16),
        jax.ShapeDtypeStruct((b, s, 2 * d), BF16),
        jax.ShapeDtypeStruct((b, s, GLA_KEY_WIDTH), F32),
    )
    return pl.pallas_call(
        _inproj_kernel,
        grid=grid,
        in_specs=[
            tok(d),
            pl.BlockSpec((1, d), lambda bi, i: (0, 0)),
            pl.BlockSpec((d, w_main_cols), lambda bi, i: (0, 0), pipeline_mode=pl.Buffered(1)),
            pl.BlockSpec((d, tail_cols), lambda bi, i: (0, w_main_cols // tail_cols),
                         pipeline_mode=pl.Buffered(1)),
        ] + [pl.BlockSpec(w.shape, lambda bi, i: (0, 0), pipeline_mode=pl.Buffered(1))
             for w in (wup, bup)],
        out_specs=(tok(SB_WIDTH), tok(SB_WIDTH), tok(SB_WIDTH), tok(2 * GLA_KEY_WIDTH), tok(GLA_VALUE_WIDTH),
                   tok(GLA_VALUE_WIDTH), tok(2 * d), tok(GLA_KEY_WIDTH)),
        out_shape=out_shape,
        scratch_shapes=[pltpu.VMEM((d, 2 * d), BF16)],
        compiler_params=pltpu.CompilerParams(
            dimension_semantics=("arbitrary", "arbitrary"), vmem_limit_bytes=VMEM_LIMIT),
        name="inproj",
    )(x, norm_g, w_in, w_in, wup, bup)


def _aligned(start, tile):
    return start if isinstance(start, int) else pl.multiple_of(start, tile)


def _sb_tile(z, neg_tri2, mask):
    tk = z.shape[1]
    f = jnp.maximum(z, 0.0) + jnp.log(1.0 + jnp.exp(-jnp.abs(z)))
    if mask is not None:
        f = jnp.where(mask, f, 0.0)
    hi = f.astype(BF16)
    lo = (f - hi.astype(F32)).astype(BF16)
    cs = jnp.dot(jnp.concatenate([hi, lo], axis=1), neg_tri2, preferred_element_type=F32)
    return z + cs[:, :tk], cs[:, tk:]


def _sb_kernel(q_ref, k_ref, v_ref, o_ref, acc_ref, carry_ref):
    s_len = q_ref.shape[1]
    heads = range(q_ref.shape[2] // SB_DIM)
    T = SB_TQ
    nq = s_len // T
    lane_head = lax.broadcasted_iota(jnp.int32, (T, q_ref.shape[2]), 1) // SB_DIM
    row = lax.broadcasted_iota(jnp.int32, (T, T), 0)
    col = lax.broadcasted_iota(jnp.int32, (T, T), 1)
    neg_tri = jnp.where(row >= col, -1.0, 0.0).astype(BF16)
    half = jnp.concatenate([neg_tri, jnp.full((T, T), -1.0, BF16)], axis=1)
    neg_tri2 = jnp.concatenate([half, half], axis=0)
    past = col < row

    r2 = lax.broadcasted_iota(jnp.int32, (2 * T, 2 * T), 0) % T
    c2 = lax.broadcasted_iota(jnp.int32, (2 * T, 2 * T), 1)
    same_head = r2 // SB_FAR == (c2 % T) // SB_FAR
    far_tri2 = jnp.where(
        jnp.logical_and(same_head, jnp.logical_or(c2 >= T, r2 >= c2)), -1.0, 0.0).astype(BF16)
    lane_head_far = lax.broadcasted_iota(jnp.int32, (SB_FAR, q_ref.shape[2]), 1) // SB_DIM

    def windows(blocks, n_near, with_far):
        chains = [(u, hh) for u in range(len(blocks)) for hh in heads]
        q0s = [_aligned(i * T, T) for i in blocks]
        k0s = [_aligned((i - (n_near - 1)) * T, T) for i in blocks]
        f0s = [_aligned((i - (n_near - 1)) * T - SB_FAR, SB_FAR) for i in blocks]
        zs = []
        for u, hh in chains:
            q = head_only(q_ref[0, pl.ds(q0s[u], T), :], hh)
            zs.append(lax.dot_general(q, k_ref[0, pl.ds(k0s[u], n_near * T), :], NT_DIMS,
                                      preferred_element_type=F32))
        z_fars = []
        if with_far:
            for u in range(len(blocks)):
                k_far = k_ref[0, pl.ds(f0s[u], SB_FAR), :]
                keys = jnp.concatenate(
                    [jnp.where(lane_head_far == hh, k_far, jnp.zeros_like(k_far)) for hh in heads], axis=0)
                z_fars.append(lax.dot_general(q_ref[0, pl.ds(q0s[u], T), :], keys, NT_DIMS,
                                              preferred_element_type=F32))
        stats = []
        for z in zs:
            per_tile = []
            for jb in reversed(range(n_near)):
                mask = past if jb == n_near - 1 else None
                per_tile.append(_sb_tile(z[:, jb * T:(jb + 1) * T], neg_tri2, mask))
            stats.append(per_tile)
        far_stats = [_sb_tile(z, far_tri2, None) for z in z_fars]
        near = [([], []) for _ in blocks]
        for (u, hh), per_tile in zip(chains, stats):
            carry = jnp.zeros((T, T), F32)
            ws = []
            for n, (logw, total) in enumerate(per_tile):
                w = jnp.exp(logw + carry)
                if n == 0:
                    w = jnp.where(past, w, 0.0)
                ws.append(w.astype(BF16))
                carry = carry + total
            w_all = ws[0] if n_near == 1 else jnp.concatenate(ws[::-1], axis=1)
            near[u][0].append(jnp.dot(w_all, v_ref[0, pl.ds(k0s[u], n_near * T), :],
                                      preferred_element_type=F32))
            near[u][1].append(carry)
        out = []
        for u, (accs, carries) in enumerate(near):
            slab, carry = by_head(accs), by_head(carries)
            if with_far:
                logw, total = far_stats[u]
                v_far = v_ref[0, pl.ds(f0s[u], SB_FAR), :]
                values = jnp.concatenate(
                    [jnp.where(lane_head_far == hh, v_far, jnp.zeros_like(v_far)) for hh in heads], axis=0)
                slab = slab + jnp.dot(jnp.exp(logw + carry).astype(BF16), values,
                                      preferred_element_type=F32)
                carry = carry + total
            out.append((accs, carries, slab, carry))
        return out

    def largest(cs):
        m = cs[0]
        for c in cs[1:]:
            m = jnp.maximum(m, c)
        return jnp.max(m)

    def head_only(a, hh):
        return jnp.where(lane_head == hh, a, jnp.zeros_like(a))

    def by_head(per_head):
        out = per_head[0]
        for hh in list(heads)[1:]:
            out = jnp.where(lane_head == hh, per_head[hh], out)
        return out

    def store(i, slab):
        o_ref[0, pl.ds(_aligned(i * T, T), T), :] = slab.astype(BF16)

    def finish(i, jb_first, accs, carries, cmax):
        q0 = _aligned(i * T, T)
        for n in range(len(accs)):
            acc_ref[n] = accs[n]
            carry_ref[n] = carries[n]

        def cond(st):
            jb, cmax = st
            return jnp.logical_and(jb >= 0, cmax > SB_UNDERFLOW)

        def body(st):
            jb, _ = st
            k0 = pl.multiple_of(jb * T, T)
            for n, hh in enumerate(heads):
                q = head_only(q_ref[0, pl.ds(q0, T), :], hh)
                z = lax.dot_general(q, k_ref[0, pl.ds(k0, T), :], NT_DIMS,
                                    preferred_element_type=F32)
                logw, total = _sb_tile(z, neg_tri2, None)
                w = jnp.exp(logw + carry_ref[n]).astype(BF16)
                acc_ref[n] += jnp.dot(w, v_ref[0, pl.ds(k0, T), :],
                                      preferred_element_type=F32)
                carry_ref[n] += total
            return jb - 1, largest([carry_ref[n] for n in range(len(accs))])

        lax.while_loop(cond, body, (jb_first, cmax))
        store(i, by_head([acc_ref[n] for n in range(len(accs))]))

    def run(blocks, n_near, with_far=False):
        results = windows(blocks, n_near, with_far)
        for i, (_, _, slab, _) in zip(blocks, results):
            store(i, slab)
        firsts = [i - n_near for i in blocks]
        if all(isinstance(jb, int) and jb < 0 for jb in firsts):
            return
        cmaxs = [jnp.max(carry) for _, _, _, carry in results]
        unfinished = [jnp.logical_and(jb >= 0, cmax > SB_UNDERFLOW) for jb, cmax in zip(firsts, cmaxs)]

        @pl.when(functools.reduce(jnp.logical_or, unfinished))
        def _():
            for i, jb, (accs, carries, _, _), cmax, more in zip(blocks, firsts, results, cmaxs, unfinished):
                @pl.when(more)
                def _(i=i, jb=jb, accs=accs, carries=carries, cmax=cmax):
                    finish(i, jnp.int32(jb), accs, carries, cmax)

    n_near = min(SB_NEAR, nq)
    for i in range(n_near - 1):
        run([i], i + 1)
    if nq > n_near - 1:
        run([n_near - 1], n_near)
    n_lead = n_near + (nq - n_near) % SB_QB
    if n_lead > n_near:
        run(list(range(n_near, n_lead)), n_near, True)

    def q_body(p, _):
        run([n_lead + p * SB_QB + u for u in range(SB_QB)], n_near, True)
        return 0

    lax.fori_loop(0, (nq - n_lead) // SB_QB, q_body, 0)


def _sb_attention(q, k, v):
    b, s, _ = q.shape
    hp = LANES // SB_DIM
    spec = pl.BlockSpec((1, s, LANES), lambda bi, h: (bi, 0, h))
    return pl.pallas_call(
        _sb_kernel,
        grid=(b, SB_HEADS // hp),
        in_specs=[spec, spec, spec],
        out_specs=spec,
        out_shape=jax.ShapeDtypeStruct((b, s, SB_WIDTH), BF16),
        scratch_shapes=[pltpu.VMEM((hp, SB_TQ, LANES), F32),
                        pltpu.VMEM((hp, SB_TQ, SB_TQ), F32)],
        compiler_params=pltpu.CompilerParams(
            dimension_semantics=("parallel", "parallel"), vmem_limit_bytes=VMEM_LIMIT),
        name="sb_attention",
    )(q, k, v)


def _rows_bcast(a, offset):
    n_sub = a.shape[0] // GLA_SUB
    return jnp.concatenate(
        [jnp.broadcast_to(a[i * GLA_SUB + offset:i * GLA_SUB + offset + 1, :], (GLA_SUB, a.shape[1]))
         for i in range(n_sub)], axis=0)


def _gla_kernel(q_ref, k_ref, v_ref, go_ref, bc_ref, gn_ref, o_ref, st_ref, raw_ref, safe_ref):
    s_len = q_ref.shape[1]
    C = GLA_CHUNK
    G = GLA_GROUP
    n_sub = C // GLA_SUB
    scale = GLA_DK ** -0.5

    r_cc = lax.broadcasted_iota(jnp.int32, (C, C), 0)
    c_cc = lax.broadcasted_iota(jnp.int32, (C, C), 1)
    causal = c_cc <= r_cc
    diag_keep = jnp.logical_and((r_cc // GLA_SUB) == (c_cc // GLA_SUB), causal)
    r_loc = lax.broadcasted_iota(jnp.int32, (C, GLA_DK), 0) % GLA_SUB
    row_pick = [(r_loc == s).astype(BF16) for s in range(GLA_SUB)]

    n_groups = s_len // (C * G)
    n_chunks = s_len // C
    spread = (bc_ref[0, pl.ds(0, n_chunks, stride=C), :]
              - bc_ref[0, pl.ds(C - 1, n_chunks, stride=C), :])
    for g in range(n_groups):
        safe_ref[g] = (jnp.max(spread[g * G:(g + 1) * G]) < GLA_SAFE_SPREAD).astype(jnp.int32)

    def scores_factored(qs, ks, bcums):
        out = []
        for q, k, bcum in zip(qs, ks, bcums):
            r0 = bcum[0:1, :]
            qn = (q * jnp.exp(bcum - r0)).astype(BF16)
            kn = (k * jnp.exp(r0 - bcum)).astype(BF16)
            sc = lax.dot_general(qn, kn, NT_DIMS, preferred_element_type=F32)
            out.append(jnp.where(causal, sc, 0.0).astype(BF16))
        return out

    def scores_bounded(qs, ks, t0s, bcums):
        offs = []
        for q, k, bcum in zip(qs, ks, bcums):
            qn = (q * jnp.exp(bcum - _rows_bcast(bcum, 0))).astype(BF16)
            off = [jnp.zeros((GLA_SUB, C), F32)]
            for i in range(1, n_sub):
                n_prev = i * GLA_SUB
                ref_row = bcum[n_prev:n_prev + 1, :]
                kn = jnp.concatenate(
                    [(k[:n_prev] * jnp.exp(ref_row - bcum[:n_prev])).astype(BF16),
                     jnp.zeros((C - n_prev, GLA_DK), BF16)], axis=0)
                off.append(lax.dot_general(qn[i * GLA_SUB:(i + 1) * GLA_SUB], kn, NT_DIMS,
                                           preferred_element_type=F32))
            offs.append(jnp.concatenate(off, axis=0))
        out = []
        for q, t0, bcum, off in zip(qs, t0s, bcums, offs):
            kb = k_ref[0, pl.ds(t0, C), :]
            prods, keys = [], []
            for s in range(GLA_SUB):
                decay = jnp.exp(jnp.minimum(bcum - _rows_bcast(bcum, s), 0.0))
                prods.append((q * decay).astype(BF16))
                keys.append(kb * row_pick[s])
            diag = lax.dot_general(jnp.concatenate(prods, axis=1), jnp.concatenate(keys, axis=1),
                                   NT_DIMS, preferred_element_type=F32)
            out.append(jnp.where(diag_keep, diag, off).astype(BF16))
        return out

    def finalize(gi):
        slot = gi % 2
        for u in range(G):
            t0 = pl.multiple_of((gi * G + u) * C, C)
            y = _rms(raw_ref[slot, u * C:(u + 1) * C, :], gn_ref[...])
            g = go_ref[0, pl.ds(t0, C), :].astype(F32)
            o_ref[0, pl.ds(t0, C), :] = (y * (g * jax.nn.sigmoid(g))).astype(BF16)

    def group(gi, factored, finalize_previous):
        if finalize_previous:
            finalize(gi - 1)
        t0s = [pl.multiple_of((gi * G + u) * C, C) for u in range(G)]
        qs = [q_ref[0, pl.ds(t0, C), :].astype(F32) * scale for t0 in t0s]
        ks = [k_ref[0, pl.ds(t0, C), :].astype(F32) for t0 in t0s]
        vs = [v_ref[0, pl.ds(t0, C), :] for t0 in t0s]
        bcums = [bc_ref[0, pl.ds(t0, C), :] for t0 in t0s]
        scores = scores_factored(qs, ks, bcums) if factored else scores_bounded(qs, ks, t0s, bcums)

        kvs, q_ins, decays = [], [], []
        for q, k, v, bcum in zip(qs, ks, vs, bcums):
            b_last = bcum[C - 1:C, :]
            k_dec = (k * jnp.exp(b_last - bcum)).astype(BF16)
            kvs.append(lax.dot_general(v, k_dec, TN_DIMS, preferred_element_type=F32))
            q_ins.append((q * jnp.exp(bcum)).astype(BF16))
            decays.append(jnp.exp(b_last))
        st = st_ref[...]
        outs = []
        for q_in, kv, decay in zip(q_ins, kvs, decays):
            outs.append(lax.dot_general(q_in, st.astype(BF16), NT_DIMS, preferred_element_type=F32))
            st = st * decay + kv
        st_ref[...] = st

        slot = gi % 2
        for u, (o, sc, v) in enumerate(zip(outs, scores, vs)):
            raw_ref[slot, u * C:(u + 1) * C, :] = o + jnp.dot(sc, v, preferred_element_type=F32)

    def main(gi, finalize_previous):
        lax.cond(safe_ref[gi] == 1, lambda: group(gi, True, finalize_previous),
                 lambda: group(gi, False, finalize_previous))

    def body(gi, _):
        main(gi, True)
        return 0

    st_ref[...] = jnp.zeros_like(st_ref)
    main(0, False)
    lax.fori_loop(1, n_groups, body, 0)
    finalize(n_groups - 1)


def _gla(gqk, gv, gout, bcum, gnorm):
    b, s, _ = gv.shape
    n_groups, rem = divmod(s, GLA_CHUNK * GLA_GROUP)
    assert rem == 0, s
    return pl.pallas_call(
        _gla_kernel,
        grid=(b, GLA_HEADS),
        in_specs=[
            pl.BlockSpec((1, s, GLA_DK), lambda bi, h: (bi, 0, h)),
            pl.BlockSpec((1, s, GLA_DK), lambda bi, h: (bi, 0, GLA_HEADS + h)),
            pl.BlockSpec((1, s, GLA_DV), lambda bi, h: (bi, 0, h)),
            pl.BlockSpec((1, s, GLA_DV), lambda bi, h: (bi, 0, h)),
            pl.BlockSpec((1, s, GLA_DK), lambda bi, h: (bi, 0, h)),
            pl.BlockSpec((1, GLA_DV), lambda bi, h: (0, 0)),
        ],
        out_specs=pl.BlockSpec((1, s, GLA_DV), lambda bi, h: (bi, 0, h)),
        out_shape=jax.ShapeDtypeStruct((b, s, GLA_VALUE_WIDTH), BF16),
        scratch_shapes=[pltpu.VMEM((GLA_DV, GLA_DK), F32),
                        pltpu.VMEM((2, GLA_CHUNK * GLA_GROUP, GLA_DV), F32),
                        pltpu.SMEM((n_groups,), jnp.int32)],
        compiler_params=pltpu.CompilerParams(
            dimension_semantics=("parallel", "parallel"), vmem_limit_bytes=VMEM_LIMIT),
        name="gla",
    )(gqk, gqk, gv, gout, bcum, gnorm)


def _tail_kernel(x_ref, osb_ref, ogla_ref, gates_ref, wpa_ref, wpb_ref, wout_ref,
                 g_ref, w1_ref, w2_ref, gf_ref, o_ref, *, ff_chunk):
    d = x_ref.shape[-1]
    a = jnp.dot(osb_ref[0], wpa_ref[...].astype(BF16), preferred_element_type=F32)
    c = jnp.dot(ogla_ref[0], wpb_ref[...].astype(BF16), preferred_element_type=F32)
    ga = jax.nn.sigmoid(gates_ref[0, :, 0:d].astype(F32))
    gb = jax.nn.sigmoid(gates_ref[0, :, d:2 * d].astype(F32))
    mix = ga * a + gb * c
    x = x_ref[0] + jnp.dot(mix.astype(BF16), wout_ref[...].astype(BF16), preferred_element_type=F32)

    inv_rms = _inv_rms(x)
    h = (x * g_ref[...]).astype(BF16)
    acc = jnp.zeros_like(x)
    for j in range(w1_ref.shape[1] // ff_chunk):
        u = jnp.dot(h, w1_ref[:, j * ff_chunk:(j + 1) * ff_chunk], preferred_element_type=F32)
        u = jnp.square(jnp.maximum(u, 0.0) * inv_rms).astype(BF16)
        acc = acc + jnp.dot(u, w2_ref[j * ff_chunk:(j + 1) * ff_chunk, :], preferred_element_type=F32)
    o_ref[0] = _rms(x + acc, gf_ref[...])


def _tail(x, osb, ogla, gates, wpa, wpb, wout, g, w1, w2, gf, tm, ff_chunk):
    b, s, d = x.shape
    tok = lambda width: pl.BlockSpec((1, tm, width), lambda bi, i: (bi, i, 0))
    vec = pl.BlockSpec((1, d), lambda bi, i: (0, 0))
    full = lambda w: pl.BlockSpec(w.shape, lambda bi, i: (0, 0), pipeline_mode=pl.Buffered(1))
    return pl.pallas_call(
        functools.partial(_tail_kernel, ff_chunk=ff_chunk),
        grid=(b, s // tm),
        in_specs=[tok(d), tok(SB_WIDTH), tok(GLA_VALUE_WIDTH), tok(2 * d), full(wpa), full(wpb),
                  full(wout), vec, full(w1), full(w2), vec],
        out_specs=tok(d),
        out_shape=jax.ShapeDtypeStruct((b, s, d), F32),
        compiler_params=pltpu.CompilerParams(
            dimension_semantics=("parallel", "parallel"), vmem_limit_bytes=VMEM_LIMIT),
        name="merge_mlp",
    )(x, osb, ogla, gates, wpa, wpb, wout, g, w1, w2, gf)


def _layer(x, norm_mix, w_in, w_gate_up, b_gate_up, gla_norm, w_proj_sb, w_proj_gla, w_out,
           norm_mlp, w_ff1, w_ff2, final_gain, tm):
    d = x.shape[-1]
    c_low = 3 * SB_WIDTH + 2 * GLA_KEY_WIDTH + 2 * GLA_VALUE_WIDTH
    wup = jnp.concatenate(
        [w_gate_up, jnp.zeros((LANES - GLA_GATE_RANK, GLA_KEY_WIDTH), w_gate_up.dtype)], axis=0).astype(BF16)

    sbq, sbk, sbv, gqk, gv, gout, gates, bcum = _inproj(
        x, norm_mix.reshape(1, d), w_in, c_low, wup, b_gate_up.reshape(1, -1), tm)
    o_sb = _sb_attention(sbq, sbk, sbv)
    o_gla = _gla(gqk, gv, gout, bcum, gla_norm.reshape(1, -1))
    return _tail(x, o_sb, o_gla, gates, w_proj_sb, w_proj_gla,
                 w_out, norm_mlp.reshape(1, d), w_ff1.astype(BF16), w_ff2.astype(BF16),
                 final_gain, tm, ff_chunk=1024)


def kernel(x, norm_mix, w_in, w_gate_up, b_gate_up, gla_norm, w_proj_sb, w_proj_gla, w_out,
           norm_mlp, w_ff1, w_ff2, norm_final):
    depth = w_in.shape[0]
    s = x.shape[1]
    tm = min(512, s)
    assert depth == 1, "the final RMSNorm is fused into the last layer's MLP kernel"
    return _layer(x, norm_mix[0], w_in[0], w_gate_up[0], b_gate_up[0], gla_norm[0], w_proj_sb[0],
                  w_proj_gla[0], w_out[0], norm_mlp[0], w_ff1[0], w_ff2[0],
                  norm_final.reshape(1, -1), tm)
```

```python
import functools

import jax
import jax.numpy as jnp
from jax import lax
from jax.experimental import pallas as pl
from jax.experimental.pallas import tpu as pltpu

F32 = jnp.float32
BF16 = jnp.bfloat16
EPS = 1e-6

SB_HEADS = 8
SB_DIM = 64
SB_WIDTH = SB_HEADS * SB_DIM
GLA_HEADS = 4
GLA_DK = 128
GLA_DV = 256
GLA_KEY_WIDTH = GLA_HEADS * GLA_DK
GLA_VALUE_WIDTH = GLA_HEADS * GLA_DV
GLA_GATE_RANK = 16
GLA_GATE_TAU = 16.0
GLA_CHUNK = 128
GLA_SUB = 16
GLA_GROUP = 4
GLA_SAFE_SPREAD = 60.0
LANES = 128

SB_TQ = 128
SB_NEAR = 2
SB_FAR = 64
SB_FAR_ROWS = 64
SB_QB = 4
SB_UNDERFLOW = -110.0

VMEM_LIMIT = 56 * 1024 * 1024

NT_DIMS = (((1,), (1,)), ((), ()))
TN_DIMS = (((0,), (0,)), ((), ()))


def _inv_rms(x):
    return lax.rsqrt(jnp.mean(x * x, axis=-1, keepdims=True) + EPS)


def _rms(x, g):
    return (x * _inv_rms(x)) * g


def _log_sigmoid(x):
    return -(jnp.maximum(-x, 0.0) + jnp.log(1.0 + jnp.exp(-jnp.abs(x))))


def _inproj_kernel(x_ref, g_ref, wa_ref, wt_ref, wup_ref, bup_ref, sbq_ref, sbk_ref, sbv_ref,
                   gqk_ref, gv_ref, gout_ref, gates_ref, bc_ref, wg_ref):
    d_model = x_ref.shape[-1]

    @pl.when(jnp.logical_and(pl.program_id(0) == 0, pl.program_id(1) == 0))
    def _():
        wg_ref[...] = wt_ref[:, GLA_GATE_RANK:GLA_GATE_RANK + 2 * d_model].astype(BF16)

    x = x_ref[0]
    inv_rms = _inv_rms(x)
    xg = (x * g_ref[...]).astype(BF16)

    def proj(w_ref, c0, width):
        w = w_ref[:, c0:c0 + width].astype(BF16)
        return jnp.dot(xg, w, preferred_element_type=F32) * inv_rms

    def sb_proj(idx, ref):
        r = proj(wa_ref, idx * SB_WIDTH, SB_WIDTH)
        if idx == 0:
            r = r * (SB_DIM ** -0.5)
        ref[0] = r.astype(BF16)

    low = proj(wt_ref, 0, LANES).astype(BF16)
    sb_proj(0, sbq_ref)
    gate = jnp.dot(low, wup_ref[...], preferred_element_type=F32) + bup_ref[...]
    sb_proj(1, sbk_ref)
    log_a = _log_sigmoid(gate) * (1.0 / GLA_GATE_TAU)
    hi = log_a.astype(BF16)
    r1 = log_a - hi.astype(F32)
    mid = r1.astype(BF16)
    lo = (r1 - mid.astype(F32)).astype(BF16)
    sb_proj(2, sbv_ref)
    c = 3 * SB_WIDTH
    gqk_ref[0] = proj(wa_ref, c, 2 * GLA_KEY_WIDTH).astype(BF16)
    c += 2 * GLA_KEY_WIDTH
    C = GLA_CHUNK
    row = lax.broadcasted_iota(jnp.int32, (C, 3 * C), 0)
    col = lax.broadcasted_iota(jnp.int32, (C, 3 * C), 1)
    ltri3 = ((col % C) <= row).astype(BF16)
    for c0 in range(0, log_a.shape[0], C):
        parts = jnp.concatenate([p[c0:c0 + C] for p in (hi, mid, lo)], axis=0)
        bc_ref[0, c0:c0 + C, :] = jnp.dot(ltri3, parts, preferred_element_type=F32)
    for ref in (gv_ref, gout_ref):
        ref[0] = proj(wa_ref, c, GLA_VALUE_WIDTH).astype(BF16)
        c += GLA_VALUE_WIDTH
    for half in range(2):
        gates_ref[0, :, half * d_model:(half + 1) * d_model] = (
            proj(wg_ref, half * d_model, d_model).astype(BF16))


def _inproj(x, norm_g, w_in, w_main_cols, wup, bup, tm):
    b, s, d = x.shape
    tail_cols = -(-(w_in.shape[1] - w_main_cols) // LANES) * LANES
    while w_main_cols % tail_cols:
        tail_cols += LANES
    assert tm % GLA_CHUNK == 0
    grid = (b, s // tm)
    tok = lambda width: pl.BlockSpec((1, tm, width), lambda bi, i: (bi, i, 0))
    out_shape = (
        jax.ShapeDtypeStruct((b, s, SB_WIDTH), BF16),
        jax.ShapeDtypeStruct((b, s, SB_WIDTH), BF16),
        jax.ShapeDtypeStruct((b, s, SB_WIDTH), BF16),
        jax.ShapeDtypeStruct((b, s, 2 * GLA_KEY_WIDTH), BF16),
        jax.ShapeDtypeStruct((b, s, GLA_VALUE_WIDTH), BF16),
        jax.ShapeDtypeStruct((b, s, GLA_VALUE_WIDTH), BF16),
        jax.ShapeDtypeStruct((b, s, 2 * d), BF16),
        jax.ShapeDtypeStruct((b, s, GLA_KEY_WIDTH), F32),
    )
    return pl.pallas_call(
        _inproj_kernel,
        grid=grid,
        in_specs=[
            tok(d),
            pl.BlockSpec((1, d), lambda bi, i: (0, 0)),
            pl.BlockSpec((d, w_main_cols), lambda bi, i: (0, 0), pipeline_mode=pl.Buffered(1)),
            pl.BlockSpec((d, tail_cols), lambda bi, i: (0, w_main_cols // tail_cols),
                         pipeline_mode=pl.Buffered(1)),
        ] + [pl.BlockSpec(w.shape, lambda bi, i: (0, 0), pipeline_mode=pl.Buffered(1))
             for w in (wup, bup)],
        out_specs=(tok(SB_WIDTH), tok(SB_WIDTH), tok(SB_WIDTH), tok(2 * GLA_KEY_WIDTH), tok(GLA_VALUE_WIDTH),
                   tok(GLA_VALUE_WIDTH), tok(2 * d), tok(GLA_KEY_WIDTH)),
        out_shape=out_shape,
        scratch_shapes=[pltpu.VMEM((d, 2 * d), BF16)],
        compiler_params=pltpu.CompilerParams(
            dimension_semantics=("arbitrary", "arbitrary"), vmem_limit_bytes=VMEM_LIMIT),
        name="inproj",
    )(x, norm_g, w_in, w_in, wup, bup)


def _aligned(start, tile):
    return start if isinstance(start, int) else pl.multiple_of(start, tile)


def _sb_tile(z, neg_tri2, mask):
    tk = z.shape[1]
    f = jnp.maximum(z, 0.0) + jnp.log(1.0 + jnp.exp(-jnp.abs(z)))
    if mask is not None:
        f = jnp.where(mask, f, 0.0)
    hi = f.astype(BF16)
    lo = (f - hi.astype(F32)).astype(BF16)
    cs = jnp.dot(jnp.concatenate([hi, lo], axis=1), neg_tri2, preferred_element_type=F32)
    return z + cs[:, :tk], cs[:, tk:]


def _sb_kernel(q_ref, k_ref, v_ref, o_ref, acc_ref, carry_ref):
    s_len = q_ref.shape[1]
    heads = range(q_ref.shape[2] // SB_DIM)
    T = SB_TQ
    nq = s_len // T
    lane_head = lax.broadcasted_iota(jnp.int32, (T, q_ref.shape[2]), 1) // SB_DIM
    row = lax.broadcasted_iota(jnp.int32, (T, T), 0)
    col = lax.broadcasted_iota(jnp.int32, (T, T), 1)
    neg_tri = jnp.where(row >= col, -1.0, 0.0).astype(BF16)
    half = jnp.concatenate([neg_tri, jnp.full((T, T), -1.0, BF16)], axis=1)
    neg_tri2 = jnp.concatenate([half, half], axis=0)
    past = col < row

    r2 = lax.broadcasted_iota(jnp.int32, (2 * T, 2 * T), 0) % T
    c2 = lax.broadcasted_iota(jnp.int32, (2 * T, 2 * T), 1)
    same_head = r2 // SB_FAR == (c2 % T) // SB_FAR
    far_tri2 = jnp.where(
        jnp.logical_and(same_head, jnp.logical_or(c2 >= T, r2 >= c2)), -1.0, 0.0).astype(BF16)
    lane_head_far = lax.broadcasted_iota(jnp.int32, (SB_FAR, q_ref.shape[2]), 1) // SB_DIM

    def windows(blocks, n_near, with_far):
        chains = [(u, hh) for u in range(len(blocks)) for hh in heads]
        q0s = [_aligned(i * T, T) for i in blocks]
        k0s = [_aligned((i - (n_near - 1)) * T, T) for i in blocks]
        f0s = [_aligned((i - (n_near - 1)) * T - SB_FAR, SB_FAR) for i in blocks]
        zs = []
        for u, hh in chains:
            q = head_only(q_ref[0, pl.ds(q0s[u], T), :], hh)
            zs.append(lax.dot_general(q, k_ref[0, pl.ds(k0s[u], n_near * T), :], NT_DIMS,
                                      preferred_element_type=F32))
        z_fars = []
        if with_far:
            for u in range(len(blocks)):
                k_far = k_ref[0, pl.ds(f0s[u], SB_FAR), :]
                keys = jnp.concatenate(
                    [jnp.where(lane_head_far == hh, k_far, jnp.zeros_like(k_far)) for hh in heads], axis=0)
                z_fars.append(lax.dot_general(q_ref[0, pl.ds(q0s[u], SB_FAR_ROWS), :], keys, NT_DIMS,
                                              preferred_element_type=F32))
        stats = []
        for z in zs:
            per_tile = []
            for jb in reversed(range(n_near)):
                mask = past if jb == n_near - 1 else None
                per_tile.append(_sb_tile(z[:, jb * T:(jb + 1) * T], neg_tri2, mask))
            stats.append(per_tile)
        far_stats = [_sb_tile(z, far_tri2, None) for z in z_fars]
        near = [([], []) for _ in blocks]
        for (u, hh), per_tile in zip(chains, stats):
            carry = jnp.zeros((T, T), F32)
            ws = []
            for n, (logw, total) in enumerate(per_tile):
                w = jnp.exp(logw + carry)
                if n == 0:
                    w = jnp.where(past, w, 0.0)
                ws.append(w.astype(BF16))
                carry = carry + total
            w_all = ws[0] if n_near == 1 else jnp.concatenate(ws[::-1], axis=1)
            near[u][0].append(jnp.dot(w_all, v_ref[0, pl.ds(k0s[u], n_near * T), :],
                                      preferred_element_type=F32))
            near[u][1].append(carry)
        out = []
        for u, (accs, carries) in enumerate(near):
            slab, carry = by_head(accs), by_head(carries)
            if with_far:
                logw, total = far_stats[u]
                v_far = v_ref[0, pl.ds(f0s[u], SB_FAR), :]
                values = jnp.concatenate(
                    [jnp.where(lane_head_far == hh, v_far, jnp.zeros_like(v_far)) for hh in heads], axis=0)
                n = SB_FAR_ROWS
                top = jnp.dot(jnp.exp(logw + carry[:n]).astype(BF16), values, preferred_element_type=F32)
                slab = jnp.concatenate([slab[:n] + top, slab[n:]], axis=0)
                carry = jnp.concatenate([carry[:n] + total, carry[n:]], axis=0)
            out.append((accs, carries, slab, carry))
        return out

    def largest(cs):
        m = cs[0]
        for c in cs[1:]:
            m = jnp.maximum(m, c)
        return jnp.max(m)

    def head_only(a, hh):
        return jnp.where(lane_head == hh, a, jnp.zeros_like(a))

    def by_head(per_head):
        out = per_head[0]
        for hh in list(heads)[1:]:
            out = jnp.where(lane_head == hh, per_head[hh], out)
        return out

    def store(i, slab):
        o_ref[0, pl.ds(_aligned(i * T, T), T), :] = slab.astype(BF16)

    def finish(i, jb_first, accs, carries, cmax):
        q0 = _aligned(i * T, T)
        for n in range(len(accs)):
            acc_ref[n] = accs[n]
            carry_ref[n] = carries[n]

        def cond(st):
            jb, cmax = st
            return jnp.logical_and(jb >= 0, cmax > SB_UNDERFLOW)

        def body(st):
            jb, _ = st
            k0 = pl.multiple_of(jb * T, T)
            for n, hh in enumerate(heads):
                q = head_only(q_ref[0, pl.ds(q0, T), :], hh)
                z = lax.dot_general(q, k_ref[0, pl.ds(k0, T), :], NT_DIMS,
                                    preferred_element_type=F32)
                logw, total = _sb_tile(z, neg_tri2, None)
                w = jnp.exp(logw + carry_ref[n]).astype(BF16)
                acc_ref[n] += jnp.dot(w, v_ref[0, pl.ds(k0, T), :],
                                      preferred_element_type=F32)
                carry_ref[n] += total
            return jb - 1, largest([carry_ref[n] for n in range(len(accs))])

        lax.while_loop(cond, body, (jb_first, cmax))
        store(i, by_head([acc_ref[n] for n in range(len(accs))]))

    def run(blocks, n_near, with_far=False):
        results = windows(blocks, n_near, with_far)
        for i, (_, _, slab, _) in zip(blocks, results):
            store(i, slab)
        firsts = [i - n_near for i in blocks]
        if all(isinstance(jb, int) and jb < 0 for jb in firsts):
            return
        cmaxs = [jnp.max(carry) for _, _, _, carry in results]
        unfinished = [jnp.logical_and(jb >= 0, cmax > SB_UNDERFLOW) for jb, cmax in zip(firsts, cmaxs)]

        @pl.when(functools.reduce(jnp.logical_or, unfinished))
        def _():
            for i, jb, (accs, carries, _, _), cmax, more in zip(blocks, firsts, results, cmaxs, unfinished):
                @pl.when(more)
                def _(i=i, jb=jb, accs=accs, carries=carries, cmax=cmax):
                    finish(i, jnp.int32(jb), accs, carries, cmax)

    n_near = min(SB_NEAR, nq)
    for i in range(n_near - 1):
        run([i], i + 1)
    if nq > n_near - 1:
        run([n_near - 1], n_near)
    n_lead = n_near + (nq - n_near) % SB_QB
    if n_lead > n_near:
        run(list(range(n_near, n_lead)), n_near, True)

    def q_body(p, _):
        run([n_lead + p * SB_QB + u for u in range(SB_QB)], n_near, True)
        return 0

    lax.fori_loop(0, (nq - n_lead) // SB_QB, q_body, 0)


def _sb_attention(q, k, v):
    b, s, _ = q.shape
    hp = LANES // SB_DIM
    spec = pl.BlockSpec((1, s, LANES), lambda bi, h: (bi, 0, h))
    return pl.pallas_call(
        _sb_kernel,
        grid=(b, SB_HEADS // hp),
        in_specs=[spec, spec, spec],
        out_specs=spec,
        out_shape=jax.ShapeDtypeStruct((b, s, SB_WIDTH), BF16),
        scratch_shapes=[pltpu.VMEM((hp, SB_TQ, LANES), F32),
                        pltpu.VMEM((hp, SB_TQ, SB_TQ), F32)],
        compiler_params=pltpu.CompilerParams(
            dimension_semantics=("parallel", "parallel"), vmem_limit_bytes=VMEM_LIMIT),
        name="sb_attention",
    )(q, k, v)


def _rows_bcast(a, offset):
    n_sub = a.shape[0] // GLA_SUB
    return jnp.concatenate(
        [jnp.broadcast_to(a[i * GLA_SUB + offset:i * GLA_SUB + offset + 1, :], (GLA_SUB, a.shape[1]))
         for i in range(n_sub)], axis=0)


def _gla_kernel(q_ref, k_ref, v_ref, go_ref, bc_ref, gn_ref, o_ref, st_ref, raw_ref, safe_ref):
    s_len = q_ref.shape[1]
    C = GLA_CHUNK
    G = GLA_GROUP
    n_sub = C // GLA_SUB
    scale = GLA_DK ** -0.5

    r_cc = lax.broadcasted_iota(jnp.int32, (C, C), 0)
    c_cc = lax.broadcasted_iota(jnp.int32, (C, C), 1)
    causal = c_cc <= r_cc
    diag_keep = jnp.logical_and((r_cc // GLA_SUB) == (c_cc // GLA_SUB), causal)
    r_loc = lax.broadcasted_iota(jnp.int32, (C, GLA_DK), 0) % GLA_SUB
    row_pick = [(r_loc == s).astype(BF16) for s in range(GLA_SUB)]

    n_groups = s_len // (C * G)
    n_chunks = s_len // C
    spread = (bc_ref[0, pl.ds(0, n_chunks, stride=C), :]
              - bc_ref[0, pl.ds(C - 1, n_chunks, stride=C), :])
    for g in range(n_groups):
        safe_ref[g] = (jnp.max(spread[g * G:(g + 1) * G]) < GLA_SAFE_SPREAD).astype(jnp.int32)

    def scores_factored(qs, ks, bcums):
        out = []
        for q, k, bcum in zip(qs, ks, bcums):
            r0 = bcum[0:1, :]
            qn = (q * jnp.exp(bcum - r0)).astype(BF16)
            kn = (k * jnp.exp(r0 - bcum)).astype(BF16)
            sc = lax.dot_general(qn, kn, NT_DIMS, preferred_element_type=F32)
            out.append(jnp.where(causal, sc, 0.0).astype(BF16))
        return out

    def scores_bounded(qs, ks, t0s, bcums):
        offs = []
        for q, k, bcum in zip(qs, ks, bcums):
            qn = (q * jnp.exp(bcum - _rows_bcast(bcum, 0))).astype(BF16)
            off = [jnp.zeros((GLA_SUB, C), F32)]
            for i in range(1, n_sub):
                n_prev = i * GLA_SUB
                ref_row = bcum[n_prev:n_prev + 1, :]
                kn = jnp.concatenate(
                    [(k[:n_prev] * jnp.exp(ref_row - bcum[:n_prev])).astype(BF16),
                     jnp.zeros((C - n_prev, GLA_DK), BF16)], axis=0)
                off.append(lax.dot_general(qn[i * GLA_SUB:(i + 1) * GLA_SUB], kn, NT_DIMS,
                                           preferred_element_type=F32))
            offs.append(jnp.concatenate(off, axis=0))
        out = []
        for q, t0, bcum, off in zip(qs, t0s, bcums, offs):
            kb = k_ref[0, pl.ds(t0, C), :]
            prods, keys = [], []
            for s in range(GLA_SUB):
                decay = jnp.exp(jnp.minimum(bcum - _rows_bcast(bcum, s), 0.0))
                prods.append((q * decay).astype(BF16))
                keys.append(kb * row_pick[s])
            diag = lax.dot_general(jnp.concatenate(prods, axis=1), jnp.concatenate(keys, axis=1),
                                   NT_DIMS, preferred_element_type=F32)
            out.append(jnp.where(diag_keep, diag, off).astype(BF16))
        return out

    def finalize(gi):
        slot = gi % 2
        for u in range(G):
            t0 = pl.multiple_of((gi * G + u) * C, C)
            y = _rms(raw_ref[slot, u * C:(u + 1) * C, :], gn_ref[...])
            g = go_ref[0, pl.ds(t0, C), :].astype(F32)
            o_ref[0, pl.ds(t0, C), :] = (y * (g * jax.nn.sigmoid(g))).astype(BF16)

    def group(gi, factored, finalize_previous):
        if finalize_previous:
            finalize(gi - 1)
        t0s = [pl.multiple_of((gi * G + u) * C, C) for u in range(G)]
        qs = [q_ref[0, pl.ds(t0, C), :].astype(F32) * scale for t0 in t0s]
        ks = [k_ref[0, pl.ds(t0, C), :].astype(F32) for t0 in t0s]
        vs = [v_ref[0, pl.ds(t0, C), :] for t0 in t0s]
        bcums = [bc_ref[0, pl.ds(t0, C), :] for t0 in t0s]
        scores = scores_factored(qs, ks, bcums) if factored else scores_bounded(qs, ks, t0s, bcums)

        kvs, q_ins, decays = [], [], []
        for q, k, v, bcum in zip(qs, ks, vs, bcums):
            b_last = bcum[C - 1:C, :]
            k_dec = (k * jnp.exp(b_last - bcum)).astype(BF16)
            kvs.append(lax.dot_general(v, k_dec, TN_DIMS, preferred_element_type=F32))
            q_ins.append((q * jnp.exp(bcum)).astype(BF16))
            decays.append(jnp.exp(b_last))
        st = st_ref[...]
        outs = []
        for q_in, kv, decay in zip(q_ins, kvs, decays):
            outs.append(lax.dot_general(q_in, st.astype(BF16), NT_DIMS, preferred_element_type=F32))
            st = st * decay + kv
        st_ref[...] = st

        slot = gi % 2
        for u, (o, sc, v) in enumerate(zip(outs, scores, vs)):
            raw_ref[slot, u * C:(u + 1) * C, :] = o + jnp.dot(sc, v, preferred_element_type=F32)

    def main(gi, finalize_previous):
        lax.cond(safe_ref[gi] == 1, lambda: group(gi, True, finalize_previous),
                 lambda: group(gi, False, finalize_previous))

    def body(gi, _):
        main(gi, True)
        return 0

    st_ref[...] = jnp.zeros_like(st_ref)
    main(0, False)
    lax.fori_loop(1, n_groups, body, 0)
    finalize(n_groups - 1)


def _gla(gqk, gv, gout, bcum, gnorm):
    b, s, _ = gv.shape
    n_groups, rem = divmod(s, GLA_CHUNK * GLA_GROUP)
    assert rem == 0, s
    return pl.pallas_call(
        _gla_kernel,
        grid=(b, GLA_HEADS),
        in_specs=[
            pl.BlockSpec((1, s, GLA_DK), lambda bi, h: (bi, 0, h)),
            pl.BlockSpec((1, s, GLA_DK), lambda bi, h: (bi, 0, GLA_HEADS + h)),
            pl.BlockSpec((1, s, GLA_DV), lambda bi, h: (bi, 0, h)),
            pl.BlockSpec((1, s, GLA_DV), lambda bi, h: (bi, 0, h)),
            pl.BlockSpec((1, s, GLA_DK), lambda bi, h: (bi, 0, h)),
            pl.BlockSpec((1, GLA_DV), lambda bi, h: (0, 0)),
        ],
        out_specs=pl.BlockSpec((1, s, GLA_DV), lambda bi, h: (bi, 0, h)),
        out_shape=jax.ShapeDtypeStruct((b, s, GLA_VALUE_WIDTH), BF16),
        scratch_shapes=[pltpu.VMEM((GLA_DV, GLA_DK), F32),
                        pltpu.VMEM((2, GLA_CHUNK * GLA_GROUP, GLA_DV), F32),
                        pltpu.SMEM((n_groups,), jnp.int32)],
        compiler_params=pltpu.CompilerParams(
            dimension_semantics=("parallel", "parallel"), vmem_limit_bytes=VMEM_LIMIT),
        name="gla",
    )(gqk, gqk, gv, gout, bcum, gnorm)


def _tail_kernel(x_ref, osb_ref, ogla_ref, gates_ref, wpa_ref, wpb_ref, wout_ref,
                 g_ref, w1_ref, w2_ref, gf_ref, o_ref, *, ff_chunk):
    d = x_ref.shape[-1]
    a = jnp.dot(osb_ref[0], wpa_ref[...].astype(BF16), preferred_element_type=F32)
    c = jnp.dot(ogla_ref[0], wpb_ref[...].astype(BF16), preferred_element_type=F32)
    ga = jax.nn.sigmoid(gates_ref[0, :, 0:d].astype(F32))
    gb = jax.nn.sigmoid(gates_ref[0, :, d:2 * d].astype(F32))
    mix = ga * a + gb * c
    x = x_ref[0] + jnp.dot(mix.astype(BF16), wout_ref[...].astype(BF16), preferred_element_type=F32)

    inv_rms = _inv_rms(x)
    h = (x * g_ref[...]).astype(BF16)
    acc = jnp.zeros_like(x)
    for j in range(w1_ref.shape[1] // ff_chunk):
        u = jnp.dot(h, w1_ref[:, j * ff_chunk:(j + 1) * ff_chunk], preferred_element_type=F32)
        u = jnp.square(jnp.maximum(u, 0.0) * inv_rms).astype(BF16)
        acc = acc + jnp.dot(u, w2_ref[j * ff_chunk:(j + 1) * ff_chunk, :], preferred_element_type=F32)
    o_ref[0] = _rms(x + acc, gf_ref[...])


def _tail(x, osb, ogla, gates, wpa, wpb, wout, g, w1, w2, gf, tm, ff_chunk):
    b, s, d = x.shape
    tok = lambda width: pl.BlockSpec((1, tm, width), lambda bi, i: (bi, i, 0))
    vec = pl.BlockSpec((1, d), lambda bi, i: (0, 0))
    full = lambda w: pl.BlockSpec(w.shape, lambda bi, i: (0, 0), pipeline_mode=pl.Buffered(1))
    return pl.pallas_call(
        functools.partial(_tail_kernel, ff_chunk=ff_chunk),
        grid=(b, s // tm),
        in_specs=[tok(d), tok(SB_WIDTH), tok(GLA_VALUE_WIDTH), tok(2 * d), full(wpa), full(wpb),
                  full(wout), vec, full(w1), full(w2), vec],
        out_specs=tok(d),
        out_shape=jax.ShapeDtypeStruct((b, s, d), F32),
        compiler_params=pltpu.CompilerParams(
            dimension_semantics=("parallel", "parallel"), vmem_limit_bytes=VMEM_LIMIT),
        name="merge_mlp",
    )(x, osb, ogla, gates, wpa, wpb, wout, g, w1, w2, gf)


def _layer(x, norm_mix, w_in, w_gate_up, b_gate_up, gla_norm, w_proj_sb, w_proj_gla, w_out,
           norm_mlp, w_ff1, w_ff2, final_gain, tm):
    d = x.shape[-1]
    c_low = 3 * SB_WIDTH + 2 * GLA_KEY_WIDTH + 2 * GLA_VALUE_WIDTH
    wup = jnp.concatenate(
        [w_gate_up, jnp.zeros((LANES - GLA_GATE_RANK, GLA_KEY_WIDTH), w_gate_up.dtype)], axis=0).astype(BF16)

    sbq, sbk, sbv, gqk, gv, gout, gates, bcum = _inproj(
        x, norm_mix.reshape(1, d), w_in, c_low, wup, b_gate_up.reshape(1, -1), tm)
    o_sb = _sb_attention(sbq, sbk, sbv)
    o_gla = _gla(gqk, gv, gout, bcum, gla_norm.reshape(1, -1))
    return _tail(x, o_sb, o_gla, gates, w_proj_sb, w_proj_gla,
                 w_out, norm_mlp.reshape(1, d), w_ff1.astype(BF16), w_ff2.astype(BF16),
                 final_gain, tm, ff_chunk=1024)


def kernel(x, norm_mix, w_in, w_gate_up, b_gate_up, gla_norm, w_proj_sb, w_proj_gla, w_out,
           norm_mlp, w_ff1, w_ff2, norm_final):
    depth = w_in.shape[0]
    s = x.shape[1]
    tm = min(512, s)
    assert depth == 1, "the final RMSNorm is fused into the last layer's MLP kernel"
    return _layer(x, norm_mix[0], w_in[0], w_gate_up[0], b_gate_up[0], gla_norm[0], w_proj_sb[0],
                  w_proj_gla[0], w_out[0], norm_mlp[0], w_ff1[0], w_ff2[0],
                  norm_final.reshape(1, -1), tm)
```

```python
import functools

import jax
import jax.numpy as jnp
from jax import lax
from jax.experimental import pallas as pl
from jax.experimental.pallas import tpu as pltpu

F32 = jnp.float32
BF16 = jnp.bfloat16
EPS = 1e-6

SB_HEADS = 8
SB_DIM = 64
SB_WIDTH = SB_HEADS * SB_DIM
GLA_HEADS = 4
GLA_DK = 128
GLA_DV = 256
GLA_KEY_WIDTH = GLA_HEADS * GLA_DK
GLA_VALUE_WIDTH = GLA_HEADS * GLA_DV
GLA_GATE_RANK = 16
GLA_GATE_TAU = 16.0
GLA_CHUNK = 128
GLA_SUB = 16
GLA_GROUP = 4
GLA_SAFE_SPREAD = 60.0
LANES = 128

SB_TQ = 128
SB_NEAR = 2
SB_FAR = 64
SB_FAR_ROWS = 64
SB_QB = 6
SB_UNDERFLOW = -110.0
SB_MASKED_LOGIT = -1e30

VMEM_LIMIT = 56 * 1024 * 1024

NT_DIMS = (((1,), (1,)), ((), ()))
TN_DIMS = (((0,), (0,)), ((), ()))


def _inv_rms(x):
    return lax.rsqrt(jnp.mean(x * x, axis=-1, keepdims=True) + EPS)


def _rms(x, g):
    return (x * _inv_rms(x)) * g


def _log_sigmoid(x):
    return -(jnp.maximum(-x, 0.0) + jnp.log(1.0 + jnp.exp(-jnp.abs(x))))


def _inproj_kernel(x_ref, g_ref, wa_ref, wt_ref, wup_ref, bup_ref, sbq_ref, sbk_ref, sbv_ref,
                   gqk_ref, gv_ref, gout_ref, gates_ref, bc_ref, wg_ref):
    d_model = x_ref.shape[-1]

    @pl.when(jnp.logical_and(pl.program_id(0) == 0, pl.program_id(1) == 0))
    def _():
        wg_ref[...] = wt_ref[:, GLA_GATE_RANK:GLA_GATE_RANK + 2 * d_model].astype(BF16)

    x = x_ref[0]
    inv_rms = _inv_rms(x)
    xg = (x * g_ref[...]).astype(BF16)

    def proj(w_ref, c0, width):
        w = w_ref[:, c0:c0 + width].astype(BF16)
        return jnp.dot(xg, w, preferred_element_type=F32) * inv_rms

    def sb_proj(idx, ref):
        r = proj(wa_ref, idx * SB_WIDTH, SB_WIDTH)
        if idx == 0:
            r = r * (SB_DIM ** -0.5)
        ref[0] = r.astype(BF16)

    low = proj(wt_ref, 0, LANES).astype(BF16)
    sb_proj(0, sbq_ref)
    gate = jnp.dot(low, wup_ref[...], preferred_element_type=F32) + bup_ref[...]
    sb_proj(1, sbk_ref)
    log_a = _log_sigmoid(gate) * (1.0 / GLA_GATE_TAU)
    hi = log_a.astype(BF16)
    r1 = log_a - hi.astype(F32)
    mid = r1.astype(BF16)
    lo = (r1 - mid.astype(F32)).astype(BF16)
    sb_proj(2, sbv_ref)
    c = 3 * SB_WIDTH
    gqk_ref[0] = proj(wa_ref, c, 2 * GLA_KEY_WIDTH).astype(BF16)
    c += 2 * GLA_KEY_WIDTH
    C = GLA_CHUNK
    row = lax.broadcasted_iota(jnp.int32, (C, 3 * C), 0)
    col = lax.broadcasted_iota(jnp.int32, (C, 3 * C), 1)
    ltri3 = ((col % C) <= row).astype(BF16)
    for c0 in range(0, log_a.shape[0], C):
        parts = jnp.concatenate([p[c0:c0 + C] for p in (hi, mid, lo)], axis=0)
        bc_ref[0, c0:c0 + C, :] = jnp.dot(ltri3, parts, preferred_element_type=F32)
    for ref in (gv_ref, gout_ref):
        ref[0] = proj(wa_ref, c, GLA_VALUE_WIDTH).astype(BF16)
        c += GLA_VALUE_WIDTH
    for half in range(2):
        gates_ref[0, :, half * d_model:(half + 1) * d_model] = (
            proj(wg_ref, half * d_model, d_model).astype(BF16))


def _inproj(x, norm_g, w_in, w_main_cols, wup, bup, tm):
    b, s, d = x.shape
    tail_cols = -(-(w_in.shape[1] - w_main_cols) // LANES) * LANES
    while w_main_cols % tail_cols:
        tail_cols += LANES
    assert tm % GLA_CHUNK == 0
    grid = (b, s // tm)
    tok = lambda width: pl.BlockSpec((1, tm, width), lambda bi, i: (bi, i, 0))
    out_shape = (
        jax.ShapeDtypeStruct((b, s, SB_WIDTH), BF16),
        jax.ShapeDtypeStruct((b, s, SB_WIDTH), BF16),
        jax.ShapeDtypeStruct((b, s, SB_WIDTH), BF16),
        jax.ShapeDtypeStruct((b, s, 2 * GLA_KEY_WIDTH), BF16),
        jax.ShapeDtypeStruct((b, s, GLA_VALUE_WIDTH), BF16),
        jax.ShapeDtypeStruct((b, s, GLA_VALUE_WIDTH), BF16),
        jax.ShapeDtypeStruct((b, s, 2 * d), BF16),
        jax.ShapeDtypeStruct((b, s, GLA_KEY_WIDTH), F32),
    )
    return pl.pallas_call(
        _inproj_kernel,
        grid=grid,
        in_specs=[
            tok(d),
            pl.BlockSpec((1, d), lambda bi, i: (0, 0)),
            pl.BlockSpec((d, w_main_cols), lambda bi, i: (0, 0), pipeline_mode=pl.Buffered(1)),
            pl.BlockSpec((d, tail_cols), lambda bi, i: (0, w_main_cols // tail_cols),
                         pipeline_mode=pl.Buffered(1)),
        ] + [pl.BlockSpec(w.shape, lambda bi, i: (0, 0), pipeline_mode=pl.Buffered(1))
             for w in (wup, bup)],
        out_specs=(tok(SB_WIDTH), tok(SB_WIDTH), tok(SB_WIDTH), tok(2 * GLA_KEY_WIDTH), tok(GLA_VALUE_WIDTH),
                   tok(GLA_VALUE_WIDTH), tok(2 * d), tok(GLA_KEY_WIDTH)),
        out_shape=out_shape,
        scratch_shapes=[pltpu.VMEM((d, 2 * d), BF16)],
        compiler_params=pltpu.CompilerParams(
            dimension_semantics=("arbitrary", "arbitrary"), vmem_limit_bytes=VMEM_LIMIT),
        name="inproj",
    )(x, norm_g, w_in, w_in, wup, bup)


def _aligned(start, tile):
    return start if isinstance(start, int) else pl.multiple_of(start, tile)


def _sb_tile(z, neg_tri2, mask):
    tk = z.shape[1]
    if mask is not None:
        z = jnp.where(mask, z, SB_MASKED_LOGIT)
    f = jnp.maximum(z, 0.0) + jnp.log(1.0 + jnp.exp(-jnp.abs(z)))
    hi = f.astype(BF16)
    lo = (f - hi.astype(F32)).astype(BF16)
    cs = jnp.dot(jnp.concatenate([hi, lo], axis=1), neg_tri2, preferred_element_type=F32)
    return z + cs[:, :tk], cs[:, tk:]


def _sb_kernel(q_ref, k_ref, v_ref, o_ref, acc_ref, carry_ref):
    s_len = q_ref.shape[1]
    heads = range(q_ref.shape[2] // SB_DIM)
    T = SB_TQ
    nq = s_len // T
    lane_head = lax.broadcasted_iota(jnp.int32, (T, q_ref.shape[2]), 1) // SB_DIM
    row = lax.broadcasted_iota(jnp.int32, (T, T), 0)
    col = lax.broadcasted_iota(jnp.int32, (T, T), 1)
    neg_tri = jnp.where(row >= col, -1.0, 0.0).astype(BF16)
    half = jnp.concatenate([neg_tri, jnp.full((T, T), -1.0, BF16)], axis=1)
    neg_tri2 = jnp.concatenate([half, half], axis=0)
    past = col < row

    r2 = lax.broadcasted_iota(jnp.int32, (2 * T, 2 * T), 0) % T
    c2 = lax.broadcasted_iota(jnp.int32, (2 * T, 2 * T), 1)
    same_head = r2 // SB_FAR == (c2 % T) // SB_FAR
    far_tri2 = jnp.where(
        jnp.logical_and(same_head, jnp.logical_or(c2 >= T, r2 >= c2)), -1.0, 0.0).astype(BF16)
    lane_head_far = lax.broadcasted_iota(jnp.int32, (SB_FAR, q_ref.shape[2]), 1) // SB_DIM

    def windows(blocks, n_near, with_far):
        chains = [(u, hh) for u in range(len(blocks)) for hh in heads]
        q0s = [_aligned(i * T, T) for i in blocks]
        k0s = [_aligned((i - (n_near - 1)) * T, T) for i in blocks]
        f0s = [_aligned((i - (n_near - 1)) * T - SB_FAR, SB_FAR) for i in blocks]
        zs = []
        for u, hh in chains:
            q = head_only(q_ref[0, pl.ds(q0s[u], T), :], hh)
            zs.append(lax.dot_general(q, k_ref[0, pl.ds(k0s[u], n_near * T), :], NT_DIMS,
                                      preferred_element_type=F32))
        z_fars = []
        if with_far:
            for u in range(len(blocks)):
                k_far = k_ref[0, pl.ds(f0s[u], SB_FAR), :]
                keys = jnp.concatenate(
                    [jnp.where(lane_head_far == hh, k_far, jnp.zeros_like(k_far)) for hh in heads], axis=0)
                z_fars.append(lax.dot_general(q_ref[0, pl.ds(q0s[u], SB_FAR_ROWS), :], keys, NT_DIMS,
                                              preferred_element_type=F32))
        stats = []
        for z in zs:
            per_tile = []
            for jb in reversed(range(n_near)):
                mask = past if jb == n_near - 1 else None
                per_tile.append(_sb_tile(z[:, jb * T:(jb + 1) * T], neg_tri2, mask))
            stats.append(per_tile)
        far_stats = [_sb_tile(z, far_tri2, None) for z in z_fars]
        near = [([], []) for _ in blocks]
        for (u, hh), per_tile in zip(chains, stats):
            carry = jnp.zeros((T, T), F32)
            ws = []
            for logw, total in per_tile:
                ws.append(jnp.exp(logw + carry).astype(BF16))
                carry = carry + total
            w_all = ws[0] if n_near == 1 else jnp.concatenate(ws[::-1], axis=1)
            near[u][0].append(jnp.dot(w_all, v_ref[0, pl.ds(k0s[u], n_near * T), :],
                                      preferred_element_type=F32))
            near[u][1].append(carry)
        out = []
        for u, (accs, carries) in enumerate(near):
            slab, carry = by_head(accs), by_head(carries)
            if with_far:
                logw, total = far_stats[u]
                v_far = v_ref[0, pl.ds(f0s[u], SB_FAR), :]
                values = jnp.concatenate(
                    [jnp.where(lane_head_far == hh, v_far, jnp.zeros_like(v_far)) for hh in heads], axis=0)
                n = SB_FAR_ROWS
                top = jnp.dot(jnp.exp(logw + carry[:n]).astype(BF16), values, preferred_element_type=F32)
                slab = jnp.concatenate([slab[:n] + top, slab[n:]], axis=0)
                carry = jnp.concatenate([carry[:n] + total, carry[n:]], axis=0)
            out.append((accs, carries, slab, carry))
        return out

    def largest(cs):
        m = cs[0]
        for c in cs[1:]:
            m = jnp.maximum(m, c)
        return jnp.max(m)

    def head_only(a, hh):
        return jnp.where(lane_head == hh, a, jnp.zeros_like(a))

    def by_head(per_head):
        out = per_head[0]
        for hh in list(heads)[1:]:
            out = jnp.where(lane_head == hh, per_head[hh], out)
        return out

    def store(i, slab):
        o_ref[0, pl.ds(_aligned(i * T, T), T), :] = slab.astype(BF16)

    def finish(i, jb_first, accs, carries, cmax):
        q0 = _aligned(i * T, T)
        for n in range(len(accs)):
            acc_ref[n] = accs[n]
            carry_ref[n] = carries[n]

        def cond(st):
            jb, cmax = st
            return jnp.logical_and(jb >= 0, cmax > SB_UNDERFLOW)

        def body(st):
            jb, _ = st
            k0 = pl.multiple_of(jb * T, T)
            for n, hh in enumerate(heads):
                q = head_only(q_ref[0, pl.ds(q0, T), :], hh)
                z = lax.dot_general(q, k_ref[0, pl.ds(k0, T), :], NT_DIMS,
                                    preferred_element_type=F32)
                logw, total = _sb_tile(z, neg_tri2, None)
                w = jnp.exp(logw + carry_ref[n]).astype(BF16)
                acc_ref[n] += jnp.dot(w, v_ref[0, pl.ds(k0, T), :],
                                      preferred_element_type=F32)
                carry_ref[n] += total
            return jb - 1, largest([carry_ref[n] for n in range(len(accs))])

        lax.while_loop(cond, body, (jb_first, cmax))
        store(i, by_head([acc_ref[n] for n in range(len(accs))]))

    def run(blocks, n_near, with_far=False):
        results = windows(blocks, n_near, with_far)
        for i, (_, _, slab, _) in zip(blocks, results):
            store(i, slab)
        firsts = [i - n_near for i in blocks]
        if all(isinstance(jb, int) and jb < 0 for jb in firsts):
            return
        cmaxs = [jnp.max(carry) for _, _, _, carry in results]
        unfinished = [jnp.logical_and(jb >= 0, cmax > SB_UNDERFLOW) for jb, cmax in zip(firsts, cmaxs)]

        @pl.when(functools.reduce(jnp.logical_or, unfinished))
        def _():
            for i, jb, (accs, carries, _, _), cmax, more in zip(blocks, firsts, results, cmaxs, unfinished):
                @pl.when(more)
                def _(i=i, jb=jb, accs=accs, carries=carries, cmax=cmax):
                    finish(i, jnp.int32(jb), accs, carries, cmax)

    n_near = min(SB_NEAR, nq)
    for i in range(n_near - 1):
        run([i], i + 1)
    if nq > n_near - 1:
        run([n_near - 1], n_near)
    n_lead = n_near + (nq - n_near) % SB_QB
    if n_lead > n_near:
        run(list(range(n_near, n_lead)), n_near, True)

    def q_body(p, _):
        run([n_lead + p * SB_QB + u for u in range(SB_QB)], n_near, True)
        return 0

    lax.fori_loop(0, (nq - n_lead) // SB_QB, q_body, 0)


def _sb_attention(q, k, v):
    b, s, _ = q.shape
    hp = LANES // SB_DIM
    assert s % SB_TQ == 0 and hp * SB_FAR == SB_TQ and SB_FAR_ROWS <= SB_TQ
    spec = pl.BlockSpec((1, s, LANES), lambda bi, h: (bi, 0, h))
    return pl.pallas_call(
        _sb_kernel,
        grid=(b, SB_HEADS // hp),
        in_specs=[spec, spec, spec],
        out_specs=spec,
        out_shape=jax.ShapeDtypeStruct((b, s, SB_WIDTH), BF16),
        scratch_shapes=[pltpu.VMEM((hp, SB_TQ, LANES), F32),
                        pltpu.VMEM((hp, SB_TQ, SB_TQ), F32)],
        compiler_params=pltpu.CompilerParams(
            dimension_semantics=("parallel", "parallel"), vmem_limit_bytes=VMEM_LIMIT),
        name="sb_attention",
    )(q, k, v)


def _rows_bcast(a, offset):
    n_sub = a.shape[0] // GLA_SUB
    return jnp.concatenate(
        [jnp.broadcast_to(a[i * GLA_SUB + offset:i * GLA_SUB + offset + 1, :], (GLA_SUB, a.shape[1]))
         for i in range(n_sub)], axis=0)


def _gla_kernel(q_ref, k_ref, v_ref, go_ref, bc_ref, gn_ref, o_ref, st_ref, raw_ref, safe_ref):
    s_len = q_ref.shape[1]
    C = GLA_CHUNK
    G = GLA_GROUP
    n_sub = C // GLA_SUB
    scale = GLA_DK ** -0.5

    r_cc = lax.broadcasted_iota(jnp.int32, (C, C), 0)
    c_cc = lax.broadcasted_iota(jnp.int32, (C, C), 1)
    causal = c_cc <= r_cc
    diag_keep = jnp.logical_and((r_cc // GLA_SUB) == (c_cc // GLA_SUB), causal)
    r_loc = lax.broadcasted_iota(jnp.int32, (C, GLA_DK), 0) % GLA_SUB
    row_pick = [(r_loc == s).astype(BF16) for s in range(GLA_SUB)]

    n_groups = s_len // (C * G)
    n_chunks = s_len // C
    spread = (bc_ref[0, pl.ds(0, n_chunks, stride=C), :]
              - bc_ref[0, pl.ds(C - 1, n_chunks, stride=C), :])
    for g in range(n_groups):
        safe_ref[g] = (jnp.max(spread[g * G:(g + 1) * G]) < GLA_SAFE_SPREAD).astype(jnp.int32)

    def scores_factored(qs, ks, bcums):
        out = []
        for q, k, bcum in zip(qs, ks, bcums):
            r0 = bcum[0:1, :]
            qn = (q * jnp.exp(bcum - r0)).astype(BF16)
            kn = (k * jnp.exp(r0 - bcum)).astype(BF16)
            sc = lax.dot_general(qn, kn, NT_DIMS, preferred_element_type=F32)
            out.append(jnp.where(causal, sc, 0.0).astype(BF16))
        return out

    def scores_bounded(qs, ks, t0s, bcums):
        offs = []
        for q, k, bcum in zip(qs, ks, bcums):
            qn = (q * jnp.exp(bcum - _rows_bcast(bcum, 0))).astype(BF16)
            off = [jnp.zeros((GLA_SUB, C), F32)]
            for i in range(1, n_sub):
                n_prev = i * GLA_SUB
                ref_row = bcum[n_prev:n_prev + 1, :]
                kn = jnp.concatenate(
                    [(k[:n_prev] * jnp.exp(ref_row - bcum[:n_prev])).astype(BF16),
                     jnp.zeros((C - n_prev, GLA_DK), BF16)], axis=0)
                off.append(lax.dot_general(qn[i * GLA_SUB:(i + 1) * GLA_SUB], kn, NT_DIMS,
                                           preferred_element_type=F32))
            offs.append(jnp.concatenate(off, axis=0))
        out = []
        for q, t0, bcum, off in zip(qs, t0s, bcums, offs):
            kb = k_ref[0, pl.ds(t0, C), :]
            prods, keys = [], []
            for s in range(GLA_SUB):
                decay = jnp.exp(jnp.minimum(bcum - _rows_bcast(bcum, s), 0.0))
                prods.append((q * decay).astype(BF16))
                keys.append(kb * row_pick[s])
            diag = lax.dot_general(jnp.concatenate(prods, axis=1), jnp.concatenate(keys, axis=1),
                                   NT_DIMS, preferred_element_type=F32)
            out.append(jnp.where(diag_keep, diag, off).astype(BF16))
        return out

    def finalize(gi):
        slot = gi % 2
        for u in range(G):
            t0 = pl.multiple_of((gi * G + u) * C, C)
            y = _rms(raw_ref[slot, u * C:(u + 1) * C, :], gn_ref[...])
            g = go_ref[0, pl.ds(t0, C), :].astype(F32)
            o_ref[0, pl.ds(t0, C), :] = (y * (g * jax.nn.sigmoid(g))).astype(BF16)

    def group(gi, factored, finalize_previous):
        if finalize_previous:
            finalize(gi - 1)
        t0s = [pl.multiple_of((gi * G + u) * C, C) for u in range(G)]
        qs = [q_ref[0, pl.ds(t0, C), :].astype(F32) * scale for t0 in t0s]
        ks = [k_ref[0, pl.ds(t0, C), :].astype(F32) for t0 in t0s]
        vs = [v_ref[0, pl.ds(t0, C), :] for t0 in t0s]
        bcums = [bc_ref[0, pl.ds(t0, C), :] for t0 in t0s]
        scores = scores_factored(qs, ks, bcums) if factored else scores_bounded(qs, ks, t0s, bcums)

        kvs, q_ins, decays = [], [], []
        for q, k, v, bcum in zip(qs, ks, vs, bcums):
            b_last = bcum[C - 1:C, :]
            k_dec = (k * jnp.exp(b_last - bcum)).astype(BF16)
            kvs.append(lax.dot_general(v, k_dec, TN_DIMS, preferred_element_type=F32))
            q_ins.append((q * jnp.exp(bcum)).astype(BF16))
            decays.append(jnp.exp(b_last))
        st = st_ref[...]
        outs = []
        for q_in, kv, decay in zip(q_ins, kvs, decays):
            outs.append(lax.dot_general(q_in, st.astype(BF16), NT_DIMS, preferred_element_type=F32))
            st = st * decay + kv
        st_ref[...] = st

        slot = gi % 2
        for u, (o, sc, v) in enumerate(zip(outs, scores, vs)):
            raw_ref[slot, u * C:(u + 1) * C, :] = o + jnp.dot(sc, v, preferred_element_type=F32)

    def main(gi, finalize_previous):
        lax.cond(safe_ref[gi] == 1, lambda: group(gi, True, finalize_previous),
                 lambda: group(gi, False, finalize_previous))

    def body(gi, _):
        main(gi, True)
        return 0

    st_ref[...] = jnp.zeros_like(st_ref)
    main(0, False)
    lax.fori_loop(1, n_groups, body, 0)
    finalize(n_groups - 1)


def _gla(gqk, gv, gout, bcum, gnorm):
    b, s, _ = gv.shape
    n_groups, rem = divmod(s, GLA_CHUNK * GLA_GROUP)
    assert rem == 0, s
    return pl.pallas_call(
        _gla_kernel,
        grid=(b, GLA_HEADS),
        in_specs=[
            pl.BlockSpec((1, s, GLA_DK), lambda bi, h: (bi, 0, h)),
            pl.BlockSpec((1, s, GLA_DK), lambda bi, h: (bi, 0, GLA_HEADS + h)),
            pl.BlockSpec((1, s, GLA_DV), lambda bi, h: (bi, 0, h)),
            pl.BlockSpec((1, s, GLA_DV), lambda bi, h: (bi, 0, h)),
            pl.BlockSpec((1, s, GLA_DK), lambda bi, h: (bi, 0, h)),
            pl.BlockSpec((1, GLA_DV), lambda bi, h: (0, 0)),
        ],
        out_specs=pl.BlockSpec((1, s, GLA_DV), lambda bi, h: (bi, 0, h)),
        out_shape=jax.ShapeDtypeStruct((b, s, GLA_VALUE_WIDTH), BF16),
        scratch_shapes=[pltpu.VMEM((GLA_DV, GLA_DK), F32),
                        pltpu.VMEM((2, GLA_CHUNK * GLA_GROUP, GLA_DV), F32),
                        pltpu.SMEM((n_groups,), jnp.int32)],
        compiler_params=pltpu.CompilerParams(
            dimension_semantics=("parallel", "parallel"), vmem_limit_bytes=VMEM_LIMIT),
        name="gla",
    )(gqk, gqk, gv, gout, bcum, gnorm)


def _tail_kernel(x_ref, osb_ref, ogla_ref, gates_ref, wpa_ref, wpb_ref, wout_ref,
                 g_ref, w1_ref, w2_ref, gf_ref, o_ref, *, ff_chunk):
    d = x_ref.shape[-1]
    a = jnp.dot(osb_ref[0], wpa_ref[...].astype(BF16), preferred_element_type=F32)
    c = jnp.dot(ogla_ref[0], wpb_ref[...].astype(BF16), preferred_element_type=F32)
    ga = jax.nn.sigmoid(gates_ref[0, :, 0:d].astype(F32))
    gb = jax.nn.sigmoid(gates_ref[0, :, d:2 * d].astype(F32))
    mix = ga * a + gb * c
    x = x_ref[0] + jnp.dot(mix.astype(BF16), wout_ref[...].astype(BF16), preferred_element_type=F32)

    inv_rms = _inv_rms(x)
    h = (x * g_ref[...]).astype(BF16)
    acc = jnp.zeros_like(x)
    for j in range(w1_ref.shape[1] // ff_chunk):
        u = jnp.dot(h, w1_ref[:, j * ff_chunk:(j + 1) * ff_chunk], preferred_element_type=F32)
        u = jnp.square(jnp.maximum(u, 0.0) * inv_rms).astype(BF16)
        acc = acc + jnp.dot(u, w2_ref[j * ff_chunk:(j + 1) * ff_chunk, :], preferred_element_type=F32)
    o_ref[0] = _rms(x + acc, gf_ref[...])


def _tail(x, osb, ogla, gates, wpa, wpb, wout, g, w1, w2, gf, tm, ff_chunk):
    b, s, d = x.shape
    tok = lambda width: pl.BlockSpec((1, tm, width), lambda bi, i: (bi, i, 0))
    vec = pl.BlockSpec((1, d), lambda bi, i: (0, 0))
    full = lambda w: pl.BlockSpec(w.shape, lambda bi, i: (0, 0), pipeline_mode=pl.Buffered(1))
    return pl.pallas_call(
        functools.partial(_tail_kernel, ff_chunk=ff_chunk),
        grid=(b, s // tm),
        in_specs=[tok(d), tok(SB_WIDTH), tok(GLA_VALUE_WIDTH), tok(2 * d), full(wpa), full(wpb),
                  full(wout), vec, full(w1), full(w2), vec],
        out_specs=tok(d),
        out_shape=jax.ShapeDtypeStruct((b, s, d), F32),
        compiler_params=pltpu.CompilerParams(
            dimension_semantics=("parallel", "parallel"), vmem_limit_bytes=VMEM_LIMIT),
        name="merge_mlp",
    )(x, osb, ogla, gates, wpa, wpb, wout, g, w1, w2, gf)


def _layer(x, norm_mix, w_in, w_gate_up, b_gate_up, gla_norm, w_proj_sb, w_proj_gla, w_out,
           norm_mlp, w_ff1, w_ff2, final_gain, tm):
    d = x.shape[-1]
    c_low = 3 * SB_WIDTH + 2 * GLA_KEY_WIDTH + 2 * GLA_VALUE_WIDTH
    wup = jnp.concatenate(
        [w_gate_up, jnp.zeros((LANES - GLA_GATE_RANK, GLA_KEY_WIDTH), w_gate_up.dtype)], axis=0).astype(BF16)

    sbq, sbk, sbv, gqk, gv, gout, gates, bcum = _inproj(
        x, norm_mix.reshape(1, d), w_in, c_low, wup, b_gate_up.reshape(1, -1), tm)
    o_sb = _sb_attention(sbq, sbk, sbv)
    o_gla = _gla(gqk, gv, gout, bcum, gla_norm.reshape(1, -1))
    return _tail(x, o_sb, o_gla, gates, w_proj_sb, w_proj_gla,
                 w_out, norm_mlp.reshape(1, d), w_ff1.astype(BF16), w_ff2.astype(BF16),
                 final_gain, tm, ff_chunk=1024)


def kernel(x, norm_mix, w_in, w_gate_up, b_gate_up, gla_norm, w_proj_sb, w_proj_gla, w_out,
           norm_mlp, w_ff1, w_ff2, norm_final):
    depth = w_in.shape[0]
    s = x.shape[1]
    tm = min(512, s)
    assert depth == 1, "the final RMSNorm is fused into the last layer's MLP kernel"
    return _layer(x, norm_mix[0], w_in[0], w_gate_up[0], b_gate_up[0], gla_norm[0], w_proj_sb[0],
                  w_proj_gla[0], w_out[0], norm_mlp[0], w_ff1[0], w_ff2[0],
                  norm_final.reshape(1, -1), tm)
```

```python
import functools

import jax
import jax.numpy as jnp
from jax import lax
from jax.experimental import pallas as pl
from jax.experimental.pallas import tpu as pltpu

F32 = jnp.float32
BF16 = jnp.bfloat16
EPS = 1e-6

SB_HEADS = 8
SB_DIM = 64
SB_WIDTH = SB_HEADS * SB_DIM
GLA_HEADS = 4
GLA_DK = 128
GLA_DV = 256
GLA_KEY_WIDTH = GLA_HEADS * GLA_DK
GLA_VALUE_WIDTH = GLA_HEADS * GLA_DV
GLA_GATE_RANK = 16
GLA_GATE_TAU = 16.0
GLA_CHUNK = 128
GLA_SUB = 16
GLA_GROUP = 4
GLA_SAFE_SPREAD = 60.0
LANES = 128

SB_TQ = 128
SB_NEAR = 2
SB_FAR = 64
SB_FAR_ROWS = 64
SB_QB = 10
SB_UNDERFLOW = -110.0
SB_MASKED_LOGIT = -1e30

VMEM_LIMIT = 56 * 1024 * 1024

NT_DIMS = (((1,), (1,)), ((), ()))
TN_DIMS = (((0,), (0,)), ((), ()))


def _inv_rms(x):
    return lax.rsqrt(jnp.mean(x * x, axis=-1, keepdims=True) + EPS)


def _rms(x, g):
    return (x * _inv_rms(x)) * g


def _log_sigmoid(x):
    return -(jnp.maximum(-x, 0.0) + jnp.log(1.0 + jnp.exp(-jnp.abs(x))))


def _inproj_kernel(x_ref, g_ref, wa_ref, wt_ref, wup_ref, bup_ref, sbq_ref, sbk_ref, sbv_ref,
                   gqk_ref, gv_ref, gout_ref, gates_ref, bc_ref, wg_ref):
    d_model = x_ref.shape[-1]

    @pl.when(jnp.logical_and(pl.program_id(0) == 0, pl.program_id(1) == 0))
    def _():
        wg_ref[...] = wt_ref[:, GLA_GATE_RANK:GLA_GATE_RANK + 2 * d_model].astype(BF16)

    x = x_ref[0]
    inv_rms = _inv_rms(x)
    xg = (x * g_ref[...]).astype(BF16)

    def proj(w_ref, c0, width):
        w = w_ref[:, c0:c0 + width].astype(BF16)
        return jnp.dot(xg, w, preferred_element_type=F32) * inv_rms

    def sb_proj(idx, ref):
        r = proj(wa_ref, idx * SB_WIDTH, SB_WIDTH)
        if idx == 0:
            r = r * (SB_DIM ** -0.5)
        ref[0] = r.astype(BF16)

    low = proj(wt_ref, 0, LANES).astype(BF16)
    sb_proj(0, sbq_ref)
    gate = jnp.dot(low, wup_ref[...], preferred_element_type=F32) + bup_ref[...]
    sb_proj(1, sbk_ref)
    log_a = _log_sigmoid(gate) * (1.0 / GLA_GATE_TAU)
    hi = log_a.astype(BF16)
    r1 = log_a - hi.astype(F32)
    mid = r1.astype(BF16)
    lo = (r1 - mid.astype(F32)).astype(BF16)
    sb_proj(2, sbv_ref)
    c = 3 * SB_WIDTH
    gqk_ref[0] = proj(wa_ref, c, 2 * GLA_KEY_WIDTH).astype(BF16)
    c += 2 * GLA_KEY_WIDTH
    C = GLA_CHUNK
    row = lax.broadcasted_iota(jnp.int32, (C, 3 * C), 0)
    col = lax.broadcasted_iota(jnp.int32, (C, 3 * C), 1)
    ltri3 = ((col % C) <= row).astype(BF16)
    for c0 in range(0, log_a.shape[0], C):
        parts = jnp.concatenate([p[c0:c0 + C] for p in (hi, mid, lo)], axis=0)
        bc_ref[0, c0:c0 + C, :] = jnp.dot(ltri3, parts, preferred_element_type=F32)
    for ref in (gv_ref, gout_ref):
        ref[0] = proj(wa_ref, c, GLA_VALUE_WIDTH).astype(BF16)
        c += GLA_VALUE_WIDTH
    for half in range(2):
        gates_ref[0, :, half * d_model:(half + 1) * d_model] = (
            proj(wg_ref, half * d_model, d_model).astype(BF16))


def _inproj(x, norm_g, w_in, w_main_cols, wup, bup, tm):
    b, s, d = x.shape
    tail_cols = -(-(w_in.shape[1] - w_main_cols) // LANES) * LANES
    while w_main_cols % tail_cols:
        tail_cols += LANES
    assert tm % GLA_CHUNK == 0
    grid = (b, s // tm)
    tok = lambda width: pl.BlockSpec((1, tm, width), lambda bi, i: (bi, i, 0))
    out_shape = (
        jax.ShapeDtypeStruct((b, s, SB_WIDTH), BF16),
        jax.ShapeDtypeStruct((b, s, SB_WIDTH), BF16),
        jax.ShapeDtypeStruct((b, s, SB_WIDTH), BF16),
        jax.ShapeDtypeStruct((b, s, 2 * GLA_KEY_WIDTH), BF16),
        jax.ShapeDtypeStruct((b, s, GLA_VALUE_WIDTH), BF16),
        jax.ShapeDtypeStruct((b, s, GLA_VALUE_WIDTH), BF16),
        jax.ShapeDtypeStruct((b, s, 2 * d), BF16),
        jax.ShapeDtypeStruct((b, s, GLA_KEY_WIDTH), F32),
    )
    return pl.pallas_call(
        _inproj_kernel,
        grid=grid,
        in_specs=[
            tok(d),
            pl.BlockSpec((1, d), lambda bi, i: (0, 0)),
            pl.BlockSpec((d, w_main_cols), lambda bi, i: (0, 0), pipeline_mode=pl.Buffered(1)),
            pl.BlockSpec((d, tail_cols), lambda bi, i: (0, w_main_cols // tail_cols),
                         pipeline_mode=pl.Buffered(1)),
        ] + [pl.BlockSpec(w.shape, lambda bi, i: (0, 0), pipeline_mode=pl.Buffered(1))
             for w in (wup, bup)],
        out_specs=(tok(SB_WIDTH), tok(SB_WIDTH), tok(SB_WIDTH), tok(2 * GLA_KEY_WIDTH), tok(GLA_VALUE_WIDTH),
                   tok(GLA_VALUE_WIDTH), tok(2 * d), tok(GLA_KEY_WIDTH)),
        out_shape=out_shape,
        scratch_shapes=[pltpu.VMEM((d, 2 * d), BF16)],
        compiler_params=pltpu.CompilerParams(
            dimension_semantics=("arbitrary", "arbitrary"), vmem_limit_bytes=VMEM_LIMIT),
        name="inproj",
    )(x, norm_g, w_in, w_in, wup, bup)


def _aligned(start, tile):
    return start if isinstance(start, int) else pl.multiple_of(start, tile)


def _sb_tile(z, neg_tri2, mask):
    tk = z.shape[1]
    if mask is not None:
        z = jnp.where(mask, z, SB_MASKED_LOGIT)
    f = jnp.maximum(z, 0.0) + jnp.log(1.0 + jnp.exp(-jnp.abs(z)))
    hi = f.astype(BF16)
    lo = (f - hi.astype(F32)).astype(BF16)
    cs = jnp.dot(jnp.concatenate([hi, lo], axis=1), neg_tri2, preferred_element_type=F32)
    return z + cs[:, :tk], cs[:, tk:]


def _sb_kernel(q_ref, k_ref, v_ref, o_ref, acc_ref, carry_ref):
    s_len = q_ref.shape[1]
    heads = range(q_ref.shape[2] // SB_DIM)
    T = SB_TQ
    nq = s_len // T
    lane_head = lax.broadcasted_iota(jnp.int32, (T, q_ref.shape[2]), 1) // SB_DIM
    row = lax.broadcasted_iota(jnp.int32, (T, T), 0)
    col = lax.broadcasted_iota(jnp.int32, (T, T), 1)
    neg_tri = jnp.where(row >= col, -1.0, 0.0).astype(BF16)
    half = jnp.concatenate([neg_tri, jnp.full((T, T), -1.0, BF16)], axis=1)
    neg_tri2 = jnp.concatenate([half, half], axis=0)
    past = col < row

    r2 = lax.broadcasted_iota(jnp.int32, (2 * T, 2 * T), 0) % T
    c2 = lax.broadcasted_iota(jnp.int32, (2 * T, 2 * T), 1)
    same_head = r2 // SB_FAR == (c2 % T) // SB_FAR
    far_tri2 = jnp.where(
        jnp.logical_and(same_head, jnp.logical_or(c2 >= T, r2 >= c2)), -1.0, 0.0).astype(BF16)
    lane_head_far = lax.broadcasted_iota(jnp.int32, (SB_FAR, q_ref.shape[2]), 1) // SB_DIM

    def windows(blocks, n_near, with_far):
        chains = [(u, hh) for u in range(len(blocks)) for hh in heads]
        q0s = [_aligned(i * T, T) for i in blocks]
        k0s = [_aligned((i - (n_near - 1)) * T, T) for i in blocks]
        f0s = [_aligned((i - (n_near - 1)) * T - SB_FAR, SB_FAR) for i in blocks]
        zs = []
        for u, hh in chains:
            q = head_only(q_ref[0, pl.ds(q0s[u], T), :], hh)
            zs.append(lax.dot_general(q, k_ref[0, pl.ds(k0s[u], n_near * T), :], NT_DIMS,
                                      preferred_element_type=F32))
        z_fars = []
        if with_far:
            for u in range(len(blocks)):
                k_far = k_ref[0, pl.ds(f0s[u], SB_FAR), :]
                keys = jnp.concatenate(
                    [jnp.where(lane_head_far == hh, k_far, jnp.zeros_like(k_far)) for hh in heads], axis=0)
                z_fars.append(lax.dot_general(q_ref[0, pl.ds(q0s[u], SB_FAR_ROWS), :], keys, NT_DIMS,
                                              preferred_element_type=F32))
        stats = []
        for z in zs:
            per_tile = []
            for jb in reversed(range(n_near)):
                mask = past if jb == n_near - 1 else None
                per_tile.append(_sb_tile(z[:, jb * T:(jb + 1) * T], neg_tri2, mask))
            stats.append(per_tile)
        far_stats = [_sb_tile(z, far_tri2, None) for z in z_fars]
        near = [([], []) for _ in blocks]
        for (u, hh), per_tile in zip(chains, stats):
            carry = jnp.zeros((T, T), F32)
            ws = []
            for logw, total in per_tile:
                ws.append(jnp.exp(logw + carry).astype(BF16))
                carry = carry + total
            w_all = ws[0] if n_near == 1 else jnp.concatenate(ws[::-1], axis=1)
            near[u][0].append(jnp.dot(w_all, v_ref[0, pl.ds(k0s[u], n_near * T), :],
                                      preferred_element_type=F32))
            near[u][1].append(carry)
        out = []
        for u, (accs, carries) in enumerate(near):
            slab, carry = by_head(accs), by_head(carries)
            if with_far:
                logw, total = far_stats[u]
                v_far = v_ref[0, pl.ds(f0s[u], SB_FAR), :]
                values = jnp.concatenate(
                    [jnp.where(lane_head_far == hh, v_far, jnp.zeros_like(v_far)) for hh in heads], axis=0)
                n = SB_FAR_ROWS
                top = jnp.dot(jnp.exp(logw + carry[:n]).astype(BF16), values, preferred_element_type=F32)
                slab = jnp.concatenate([slab[:n] + top, slab[n:]], axis=0)
                carry = jnp.concatenate([carry[:n] + total, carry[n:]], axis=0)
            out.append((accs, carries, slab, carry))
        return out

    def largest(cs):
        m = cs[0]
        for c in cs[1:]:
            m = jnp.maximum(m, c)
        return jnp.max(m)

    def head_only(a, hh):
        return jnp.where(lane_head == hh, a, jnp.zeros_like(a))

    def by_head(per_head):
        out = per_head[0]
        for hh in list(heads)[1:]:
            out = jnp.where(lane_head == hh, per_head[hh], out)
        return out

    def store(i, slab):
        o_ref[0, pl.ds(_aligned(i * T, T), T), :] = slab.astype(BF16)

    def finish(i, jb_first, accs, carries, cmax):
        q0 = _aligned(i * T, T)
        for n in range(len(accs)):
            acc_ref[n] = accs[n]
            carry_ref[n] = carries[n]

        def cond(st):
            jb, cmax = st
            return jnp.logical_and(jb >= 0, cmax > SB_UNDERFLOW)

        def body(st):
            jb, _ = st
            k0 = pl.multiple_of(jb * T, T)
            for n, hh in enumerate(heads):
                q = head_only(q_ref[0, pl.ds(q0, T), :], hh)
                z = lax.dot_general(q, k_ref[0, pl.ds(k0, T), :], NT_DIMS,
                                    preferred_element_type=F32)
                logw, total = _sb_tile(z, neg_tri2, None)
                w = jnp.exp(logw + carry_ref[n]).astype(BF16)
                acc_ref[n] += jnp.dot(w, v_ref[0, pl.ds(k0, T), :],
                                      preferred_element_type=F32)
                carry_ref[n] += total
            return jb - 1, largest([carry_ref[n] for n in range(len(accs))])

        lax.while_loop(cond, body, (jb_first, cmax))
        store(i, by_head([acc_ref[n] for n in range(len(accs))]))

    def run(blocks, n_near, with_far=False):
        results = windows(blocks, n_near, with_far)
        for i, (_, _, slab, _) in zip(blocks, results):
            store(i, slab)
        firsts = [i - n_near for i in blocks]
        if all(isinstance(jb, int) and jb < 0 for jb in firsts):
            return
        cmaxs = [jnp.max(carry) for _, _, _, carry in results]
        unfinished = [jnp.logical_and(jb >= 0, cmax > SB_UNDERFLOW) for jb, cmax in zip(firsts, cmaxs)]

        @pl.when(functools.reduce(jnp.logical_or, unfinished))
        def _():
            for i, jb, (accs, carries, _, _), cmax, more in zip(blocks, firsts, results, cmaxs, unfinished):
                @pl.when(more)
                def _(i=i, jb=jb, accs=accs, carries=carries, cmax=cmax):
                    finish(i, jnp.int32(jb), accs, carries, cmax)

    n_near = min(SB_NEAR, nq)
    for i in range(n_near - 1):
        run([i], i + 1)
    if nq > n_near - 1:
        run([n_near - 1], n_near)
    n_lead = n_near + (nq - n_near) % SB_QB
    if n_lead > n_near:
        run(list(range(n_near, n_lead)), n_near, True)

    def q_body(p, _):
        run([n_lead + p * SB_QB + u for u in range(SB_QB)], n_near, True)
        return 0

    lax.fori_loop(0, (nq - n_lead) // SB_QB, q_body, 0)


def _sb_attention(q, k, v):
    b, s, _ = q.shape
    hp = LANES // SB_DIM
    assert s % SB_TQ == 0 and hp * SB_FAR == SB_TQ and SB_FAR_ROWS <= SB_TQ
    spec = pl.BlockSpec((1, s, LANES), lambda bi, h: (bi, 0, h))
    return pl.pallas_call(
        _sb_kernel,
        grid=(b, SB_HEADS // hp),
        in_specs=[spec, spec, spec],
        out_specs=spec,
        out_shape=jax.ShapeDtypeStruct((b, s, SB_WIDTH), BF16),
        scratch_shapes=[pltpu.VMEM((hp, SB_TQ, LANES), F32),
                        pltpu.VMEM((hp, SB_TQ, SB_TQ), F32)],
        compiler_params=pltpu.CompilerParams(
            dimension_semantics=("parallel", "parallel"), vmem_limit_bytes=VMEM_LIMIT),
        name="sb_attention",
    )(q, k, v)


def _rows_bcast(a, offset):
    n_sub = a.shape[0] // GLA_SUB
    return jnp.concatenate(
        [jnp.broadcast_to(a[i * GLA_SUB + offset:i * GLA_SUB + offset + 1, :], (GLA_SUB, a.shape[1]))
         for i in range(n_sub)], axis=0)


def _gla_kernel(q_ref, k_ref, v_ref, go_ref, bc_ref, gn_ref, o_ref, st_ref, raw_ref, safe_ref):
    s_len = q_ref.shape[1]
    C = GLA_CHUNK
    G = GLA_GROUP
    n_sub = C // GLA_SUB
    scale = GLA_DK ** -0.5

    r_cc = lax.broadcasted_iota(jnp.int32, (C, C), 0)
    c_cc = lax.broadcasted_iota(jnp.int32, (C, C), 1)
    causal = c_cc <= r_cc
    diag_keep = jnp.logical_and((r_cc // GLA_SUB) == (c_cc // GLA_SUB), causal)
    r_loc = lax.broadcasted_iota(jnp.int32, (C, GLA_DK), 0) % GLA_SUB
    row_pick = [(r_loc == s).astype(BF16) for s in range(GLA_SUB)]

    n_groups = s_len // (C * G)
    n_chunks = s_len // C
    spread = (bc_ref[0, pl.ds(0, n_chunks, stride=C), :]
              - bc_ref[0, pl.ds(C - 1, n_chunks, stride=C), :])
    for g in range(n_groups):
        safe_ref[g] = (jnp.max(spread[g * G:(g + 1) * G]) < GLA_SAFE_SPREAD).astype(jnp.int32)

    def scores_factored(qs, ks, bcums):
        out = []
        for q, k, bcum in zip(qs, ks, bcums):
            r0 = bcum[0:1, :]
            qn = (q * jnp.exp(bcum - r0)).astype(BF16)
            kn = (k * jnp.exp(r0 - bcum)).astype(BF16)
            sc = lax.dot_general(qn, kn, NT_DIMS, preferred_element_type=F32)
            out.append(jnp.where(causal, sc, 0.0).astype(BF16))
        return out

    def scores_bounded(qs, ks, t0s, bcums):
        offs = []
        for q, k, bcum in zip(qs, ks, bcums):
            qn = (q * jnp.exp(bcum - _rows_bcast(bcum, 0))).astype(BF16)
            off = [jnp.zeros((GLA_SUB, C), F32)]
            for i in range(1, n_sub):
                n_prev = i * GLA_SUB
                ref_row = bcum[n_prev:n_prev + 1, :]
                kn = jnp.concatenate(
                    [(k[:n_prev] * jnp.exp(ref_row - bcum[:n_prev])).astype(BF16),
                     jnp.zeros((C - n_prev, GLA_DK), BF16)], axis=0)
                off.append(lax.dot_general(qn[i * GLA_SUB:(i + 1) * GLA_SUB], kn, NT_DIMS,
                                           preferred_element_type=F32))
            offs.append(jnp.concatenate(off, axis=0))
        out = []
        for q, t0, bcum, off in zip(qs, t0s, bcums, offs):
            kb = k_ref[0, pl.ds(t0, C), :]
            prods, keys = [], []
            for s in range(GLA_SUB):
                decay = jnp.exp(jnp.minimum(bcum - _rows_bcast(bcum, s), 0.0))
                prods.append((q * decay).astype(BF16))
                keys.append(kb * row_pick[s])
            diag = lax.dot_general(jnp.concatenate(prods, axis=1), jnp.concatenate(keys, axis=1),
                                   NT_DIMS, preferred_element_type=F32)
            out.append(jnp.where(diag_keep, diag, off).astype(BF16))
        return out

    def finalize(gi):
        slot = gi % 2
        for u in range(G):
            t0 = pl.multiple_of((gi * G + u) * C, C)
            y = _rms(raw_ref[slot, u * C:(u + 1) * C, :], gn_ref[...])
            g = go_ref[0, pl.ds(t0, C), :].astype(F32)
            o_ref[0, pl.ds(t0, C), :] = (y * (g * jax.nn.sigmoid(g))).astype(BF16)

    def group(gi, factored, finalize_previous):
        if finalize_previous:
            finalize(gi - 1)
        t0s = [pl.multiple_of((gi * G + u) * C, C) for u in range(G)]
        qs = [q_ref[0, pl.ds(t0, C), :].astype(F32) * scale for t0 in t0s]
        ks = [k_ref[0, pl.ds(t0, C), :].astype(F32) for t0 in t0s]
        vs = [v_ref[0, pl.ds(t0, C), :] for t0 in t0s]
        bcums = [bc_ref[0, pl.ds(t0, C), :] for t0 in t0s]
        scores = scores_factored(qs, ks, bcums) if factored else scores_bounded(qs, ks, t0s, bcums)

        kvs, q_ins, decays = [], [], []
        for q, k, v, bcum in zip(qs, ks, vs, bcums):
            b_last = bcum[C - 1:C, :]
            k_dec = (k * jnp.exp(b_last - bcum)).astype(BF16)
            kvs.append(lax.dot_general(v, k_dec, TN_DIMS, preferred_element_type=F32))
            q_ins.append((q * jnp.exp(bcum)).astype(BF16))
            decays.append(jnp.exp(b_last))
        st = st_ref[...]
        outs = []
        for q_in, kv, decay in zip(q_ins, kvs, decays):
            outs.append(lax.dot_general(q_in, st.astype(BF16), NT_DIMS, preferred_element_type=F32))
            st = st * decay + kv
        st_ref[...] = st

        slot = gi % 2
        for u, (o, sc, v) in enumerate(zip(outs, scores, vs)):
            raw_ref[slot, u * C:(u + 1) * C, :] = o + jnp.dot(sc, v, preferred_element_type=F32)

    def main(gi, finalize_previous):
        lax.cond(safe_ref[gi] == 1, lambda: group(gi, True, finalize_previous),
                 lambda: group(gi, False, finalize_previous))

    def body(gi, _):
        main(gi, True)
        return 0

    st_ref[...] = jnp.zeros_like(st_ref)
    main(0, False)
    lax.fori_loop(1, n_groups, body, 0)
    finalize(n_groups - 1)


def _gla(gqk, gv, gout, bcum, gnorm):
    b, s, _ = gv.shape
    n_groups, rem = divmod(s, GLA_CHUNK * GLA_GROUP)
    assert rem == 0, s
    return pl.pallas_call(
        _gla_kernel,
        grid=(b, GLA_HEADS),
        in_specs=[
            pl.BlockSpec((1, s, GLA_DK), lambda bi, h: (bi, 0, h)),
            pl.BlockSpec((1, s, GLA_DK), lambda bi, h: (bi, 0, GLA_HEADS + h)),
            pl.BlockSpec((1, s, GLA_DV), lambda bi, h: (bi, 0, h)),
            pl.BlockSpec((1, s, GLA_DV), lambda bi, h: (bi, 0, h)),
            pl.BlockSpec((1, s, GLA_DK), lambda bi, h: (bi, 0, h)),
            pl.BlockSpec((1, GLA_DV), lambda bi, h: (0, 0)),
        ],
        out_specs=pl.BlockSpec((1, s, GLA_DV), lambda bi, h: (bi, 0, h)),
        out_shape=jax.ShapeDtypeStruct((b, s, GLA_VALUE_WIDTH), BF16),
        scratch_shapes=[pltpu.VMEM((GLA_DV, GLA_DK), F32),
                        pltpu.VMEM((2, GLA_CHUNK * GLA_GROUP, GLA_DV), F32),
                        pltpu.SMEM((n_groups,), jnp.int32)],
        compiler_params=pltpu.CompilerParams(
            dimension_semantics=("parallel", "parallel"), vmem_limit_bytes=VMEM_LIMIT),
        name="gla",
    )(gqk, gqk, gv, gout, bcum, gnorm)


def _tail_kernel(x_ref, osb_ref, ogla_ref, gates_ref, wpa_ref, wpb_ref, wout_ref,
                 g_ref, w1_ref, w2_ref, gf_ref, o_ref, *, ff_chunk):
    d = x_ref.shape[-1]
    a = jnp.dot(osb_ref[0], wpa_ref[...].astype(BF16), preferred_element_type=F32)
    c = jnp.dot(ogla_ref[0], wpb_ref[...].astype(BF16), preferred_element_type=F32)
    ga = jax.nn.sigmoid(gates_ref[0, :, 0:d].astype(F32))
    gb = jax.nn.sigmoid(gates_ref[0, :, d:2 * d].astype(F32))
    mix = ga * a + gb * c
    x = x_ref[0] + jnp.dot(mix.astype(BF16), wout_ref[...].astype(BF16), preferred_element_type=F32)

    inv_rms = _inv_rms(x)
    h = (x * g_ref[...]).astype(BF16)
    acc = jnp.zeros_like(x)
    for j in range(w1_ref.shape[1] // ff_chunk):
        u = jnp.dot(h, w1_ref[:, j * ff_chunk:(j + 1) * ff_chunk], preferred_element_type=F32)
        u = jnp.square(jnp.maximum(u, 0.0) * inv_rms).astype(BF16)
        acc = acc + jnp.dot(u, w2_ref[j * ff_chunk:(j + 1) * ff_chunk, :], preferred_element_type=F32)
    o_ref[0] = _rms(x + acc, gf_ref[...])


def _tail(x, osb, ogla, gates, wpa, wpb, wout, g, w1, w2, gf, tm, ff_chunk):
    b, s, d = x.shape
    tok = lambda width: pl.BlockSpec((1, tm, width), lambda bi, i: (bi, i, 0))
    vec = pl.BlockSpec((1, d), lambda bi, i: (0, 0))
    full = lambda w: pl.BlockSpec(w.shape, lambda bi, i: (0, 0), pipeline_mode=pl.Buffered(1))
    return pl.pallas_call(
        functools.partial(_tail_kernel, ff_chunk=ff_chunk),
        grid=(b, s // tm),
        in_specs=[tok(d), tok(SB_WIDTH), tok(GLA_VALUE_WIDTH), tok(2 * d), full(wpa), full(wpb),
                  full(wout), vec, full(w1), full(w2), vec],
        out_specs=tok(d),
        out_shape=jax.ShapeDtypeStruct((b, s, d), F32),
        compiler_params=pltpu.CompilerParams(
            dimension_semantics=("parallel", "parallel"), vmem_limit_bytes=VMEM_LIMIT),
        name="merge_mlp",
    )(x, osb, ogla, gates, wpa, wpb, wout, g, w1, w2, gf)


def _layer(x, norm_mix, w_in, w_gate_up, b_gate_up, gla_norm, w_proj_sb, w_proj_gla, w_out,
           norm_mlp, w_ff1, w_ff2, final_gain, tm):
    d = x.shape[-1]
    c_low = 3 * SB_WIDTH + 2 * GLA_KEY_WIDTH + 2 * GLA_VALUE_WIDTH
    wup = jnp.concatenate(
        [w_gate_up, jnp.zeros((LANES - GLA_GATE_RANK, GLA_KEY_WIDTH), w_gate_up.dtype)], axis=0).astype(BF16)

    sbq, sbk, sbv, gqk, gv, gout, gates, bcum = _inproj(
        x, norm_mix.reshape(1, d), w_in, c_low, wup, b_gate_up.reshape(1, -1), tm)
    o_sb = _sb_attention(sbq, sbk, sbv)
    o_gla = _gla(gqk, gv, gout, bcum, gla_norm.reshape(1, -1))
    return _tail(x, o_sb, o_gla, gates, w_proj_sb, w_proj_gla,
                 w_out, norm_mlp.reshape(1, d), w_ff1.astype(BF16), w_ff2.astype(BF16),
                 final_gain, tm, ff_chunk=1024)


def kernel(x, norm_mix, w_in, w_gate_up, b_gate_up, gla_norm, w_proj_sb, w_proj_gla, w_out,
           norm_mlp, w_ff1, w_ff2, norm_final):
    depth = w_in.shape[0]
    s = x.shape[1]
    tm = min(512, s)
    assert depth == 1, "the final RMSNorm is fused into the last layer's MLP kernel"
    return _layer(x, norm_mix[0], w_in[0], w_gate_up[0], b_gate_up[0], gla_norm[0], w_proj_sb[0],
                  w_proj_gla[0], w_out[0], norm_mlp[0], w_ff1[0], w_ff2[0],
                  norm_final.reshape(1, -1), tm)
```
